```python
import math
import jax, jax.numpy as jnp
from jax import lax
import numpy as np

D_MODEL = 2048
BATCH = 2
SEQ = 8192
DEPTH = 1

ATT_HEADS = 8
ATT_HEAD_DIM = 64
ATT_V_DIM = 2 * ATT_HEAD_DIM
ATT_QK_WIDTH = ATT_HEADS * 2 * ATT_HEAD_DIM
ATT_WIDTH = ATT_HEADS * ATT_V_DIM
ROPE_DIM = ATT_HEAD_DIM // 4
ROPE_THETA = 500000.0
Q_BLOCK = 128

HY_WIDTH = 1024
HY_ORDER = 2
HY_SHORT_CONV = 3
HY_EMB_BANDS = 16
HY_EMB_DIM = 1 + 2 * HY_EMB_BANDS
HY_FILTER_HIDDEN = 64
HY_DECAY_TARGET = 1e-2
HY_FAST_DECAY_PCT = 0.3
HY_SLOW_DECAY_PCT = 1.5
HY_MIN_DECAY = math.log(HY_DECAY_TARGET) / HY_FAST_DECAY_PCT
HY_MAX_DECAY = math.log(HY_DECAY_TARGET) / HY_SLOW_DECAY_PCT

N_BRANCHES = 2
W_IN_COLS = (HY_ORDER + 1) * HY_WIDTH + 2 * ATT_QK_WIDTH + ATT_WIDTH + N_BRANCHES * D_MODEL

FFN_HIDDEN = (((8 * D_MODEL + 2) // 3 + 255) // 256) * 256

N_MOD = 6
EPS = 1e-6

kernel_name = "hybrid_hyena_diffattn_encoder_block"


def rmsnorm(x, g):
    xf = x.astype(jnp.float32)
    y = xf * lax.rsqrt(jnp.mean(xf * xf, axis=-1, keepdims=True) + EPS)
    return (y * g).astype(x.dtype)


def short_conv_centred(u, w, b):
    L = u.shape[1]
    pad = HY_SHORT_CONV // 2
    up = jnp.pad(u, ((0, 0), (pad, pad), (0, 0)))
    out = up[:, 0:L] * w[0]
    for i in range(1, HY_SHORT_CONV):
        out = out + up[:, i:i + L] * w[i]
    return out + b


def hyena_filters(L, w1, b1, w2, b2, w3, b3, freq, w_out):
    t = jnp.linspace(0.0, 1.0, L, dtype=jnp.float32)[:, None]
    w = (2.0 * math.pi / L) * jnp.arange(L, dtype=jnp.float32)[:, None]
    bands = jnp.linspace(1e-4, HY_EMB_BANDS - 1, HY_EMB_BANDS, dtype=jnp.float32)
    z = jnp.concatenate([t, jnp.cos(bands * w), -jnp.sin(bands * w)], axis=-1)
    h = jnp.sin(freq[0] * (z @ w1 + b1))
    h = jnp.sin(freq[1] * (h @ w2 + b2))
    h = jnp.sin(freq[2] * (h @ w3 + b3))
    h = (h @ w_out).reshape(L, HY_ORDER, 2, HY_WIDTH)
    deltas = jnp.abs(jnp.linspace(HY_MIN_DECAY, HY_MAX_DECAY, HY_WIDTH, dtype=jnp.float32))
    decay = jnp.exp(-t * deltas)
    return h * decay[:, None, None, :]


def bidir_long_conv(u, h_fwd, h_bwd, bias):
    L, C = h_fwd.shape
    k = jnp.concatenate([h_fwd, jnp.zeros((1, C), h_fwd.dtype), h_bwd[1:][::-1]], axis=0)
    uf = jnp.fft.rfft(u.astype(jnp.float32), n=2 * L, axis=1)
    kf = jnp.fft.rfft(k.astype(jnp.float32), n=2 * L, axis=0)
    y = jnp.fft.irfft(uf * kf[None], n=2 * L, axis=1)[:, :L]
    return (y + u * bias).astype(u.dtype)


def partial_rope(x, pos):
    half = ROPE_DIM // 2
    inv = ROPE_THETA ** (-jnp.arange(half, dtype=jnp.float32) * (2.0 / ROPE_DIM))
    ang = pos.astype(jnp.float32)[:, :, None] * inv
    cos = jnp.cos(ang)[:, :, None, None, :]
    sin = jnp.sin(ang)[:, :, None, None, :]
    xr = x[..., :ROPE_DIM].astype(jnp.float32)
    x1, x2 = xr[..., :half], xr[..., half:]
    rot = jnp.concatenate([x1 * cos - x2 * sin, x2 * cos + x1 * sin], axis=-1)
    return jnp.concatenate([rot.astype(x.dtype), x[..., ROPE_DIM:]], axis=-1)


def diff_attention(q, k, v, pos, q_g, k_g, lam, subln_g, lam_init):
    B, L = q.shape[0], q.shape[1]
    q = partial_rope(rmsnorm(q, q_g), pos) * (ATT_HEAD_DIM ** -0.5)
    k = partial_rope(rmsnorm(k, k_g), pos)
    qh = q.reshape(B, L, 2 * ATT_HEADS, ATT_HEAD_DIM).transpose(0, 2, 1, 3)
    kh = k.reshape(B, L, 2 * ATT_HEADS, ATT_HEAD_DIM).transpose(0, 2, 1, 3)
    vh = v.transpose(0, 2, 1, 3)
    nb = L // Q_BLOCK
    qb = qh.reshape(B, 2 * ATT_HEADS, nb, Q_BLOCK, ATT_HEAD_DIM).transpose(2, 0, 1, 3, 4)

    def block(q_blk):
        s = jnp.einsum('bhqd,bhkd->bhqk', q_blk, kh, preferred_element_type=jnp.float32)
        p = jax.nn.softmax(s, axis=-1).reshape(B, ATT_HEADS, 2, Q_BLOCK, L)
        a = p[:, :, 0] - lam * p[:, :, 1]
        return jnp.einsum('bhqk,bhkv->bhqv', a, vh, preferred_element_type=jnp.float32)

    o = lax.map(block, qb)
    o = o.transpose(1, 0, 3, 2, 4).reshape(B, L, ATT_HEADS, ATT_V_DIM)
    o = rmsnorm(o, subln_g) * (1.0 - lam_init)
    return o.reshape(B, L, ATT_WIDTH).astype(v.dtype)


def setup_inputs(seed: int = 0) -> dict:
    key = jax.random.key(seed)
    ks = jax.random.split(key, 32)
    f32 = jnp.float32

    def nrm(k, shape, scale):
        return jax.random.normal(k, shape, f32) * scale

    Dp = DEPTH
    return {
        "x": nrm(ks[0], (BATCH, SEQ, D_MODEL), 1.0),
        "c": nrm(ks[1], (BATCH, D_MODEL), 1.0),
        "positions": (jnp.arange(SEQ, dtype=jnp.int32)[None, :]
                      + jax.random.randint(ks[2], (BATCH, 1), 0, 4096, dtype=jnp.int32)),
        "w_ada": nrm(ks[3], (Dp, D_MODEL, N_MOD * D_MODEL), 0.5 * D_MODEL ** -0.5),
        "b_ada": nrm(ks[4], (Dp, N_MOD * D_MODEL), 0.01),
        "norm1_g": 1.0 + nrm(ks[5], (Dp, D_MODEL), 0.02),
        "w_in": nrm(ks[6], (Dp, D_MODEL, W_IN_COLS), D_MODEL ** -0.5),
        "hy_conv_w": nrm(ks[7], (Dp, HY_SHORT_CONV, (HY_ORDER + 1) * HY_WIDTH), HY_SHORT_CONV ** -0.5),
        "hy_conv_b": nrm(ks[8], (Dp, (HY_ORDER + 1) * HY_WIDTH), 0.01),
        "hy_filt_w1": nrm(ks[9], (Dp, HY_EMB_DIM, HY_FILTER_HIDDEN), HY_EMB_DIM ** -0.5),
        "hy_filt_b1": nrm(ks[10], (Dp, HY_FILTER_HIDDEN), 0.1),
        "hy_filt_w2": nrm(ks[11], (Dp, HY_FILTER_HIDDEN, HY_FILTER_HIDDEN), HY_FILTER_HIDDEN ** -0.5),
        "hy_filt_b2": nrm(ks[12], (Dp, HY_FILTER_HIDDEN), 0.1),
        "hy_filt_w3": nrm(ks[13], (Dp, HY_FILTER_HIDDEN, HY_FILTER_HIDDEN), HY_FILTER_HIDDEN ** -0.5),
        "hy_filt_b3": nrm(ks[14], (Dp, HY_FILTER_HIDDEN), 0.1),
        "hy_filt_freq": 1.0 + nrm(ks[15], (Dp, 3, HY_FILTER_HIDDEN), 0.1),
        "hy_filt_w_out": nrm(ks[16], (Dp, HY_FILTER_HIDDEN, HY_ORDER * 2 * HY_WIDTH), 0.03 * HY_FILTER_HIDDEN ** -0.5),
        "hy_bias": nrm(ks[17], (Dp, HY_ORDER, HY_WIDTH), 1.0),
        "q_norm_g": 1.0 + nrm(ks[18], (Dp, ATT_HEAD_DIM), 0.02),
        "k_norm_g": 1.0 + nrm(ks[19], (Dp, ATT_HEAD_DIM), 0.02),
        "lam_q1": nrm(ks[20], (Dp, ATT_HEAD_DIM), 0.1),
        "lam_k1": nrm(ks[21], (Dp, ATT_HEAD_DIM), 0.1),
        "lam_q2": nrm(ks[22], (Dp, ATT_HEAD_DIM), 0.1),
        "lam_k2": nrm(ks[23], (Dp, ATT_HEAD_DIM), 0.1),
        "subln_g": 1.0 + nrm(ks[24], (Dp, ATT_V_DIM), 0.02),
        "w_proj_hy": nrm(ks[25], (Dp, HY_WIDTH, D_MODEL), HY_WIDTH ** -0.5),
        "w_proj_att": nrm(ks[26], (Dp, ATT_WIDTH, D_MODEL), ATT_WIDTH ** -0.5),
        "w_out": nrm(ks[27], (Dp, D_MODEL, D_MODEL), D_MODEL ** -0.5),
        "norm2_g": 1.0 + nrm(ks[28], (Dp, D_MODEL), 0.02),
        "w_gate": nrm(ks[29], (Dp, D_MODEL, FFN_HIDDEN), D_MODEL ** -0.5),
        "w_up": nrm(ks[30], (Dp, D_MODEL, FFN_HIDDEN), D_MODEL ** -0.5),
        "w_down": nrm(ks[31], (Dp, FFN_HIDDEN, D_MODEL), FFN_HIDDEN ** -0.5),
    }


def reference(x, c, positions, w_ada, b_ada, norm1_g, w_in, hy_conv_w, hy_conv_b,
              hy_filt_w1, hy_filt_b1, hy_filt_w2, hy_filt_b2, hy_filt_w3, hy_filt_b3,
              hy_filt_freq, hy_filt_w_out, hy_bias, q_norm_g, k_norm_g,
              lam_q1, lam_k1, lam_q2, lam_k2, subln_g, w_proj_hy, w_proj_att, w_out,
              norm2_g, w_gate, w_up, w_down):
    B, L, _ = x.shape
    s_hy = (HY_ORDER + 1) * HY_WIDTH
    s_q = s_hy + ATT_QK_WIDTH
    s_k = s_q + ATT_QK_WIDTH
    s_v = s_k + ATT_WIDTH
    for l in range(DEPTH):
        lam_init = 0.8 - 0.6 * math.exp(-0.3 * l)
        mod = (jax.nn.silu(c) @ w_ada[l] + b_ada[l])[:, None, :]
        sh1, sc1, g1, sh2, sc2, g2 = jnp.split(mod, N_MOD, axis=-1)

        h = rmsnorm(x, norm1_g[l]) * (1.0 + sc1) + sh1
        proj = h @ w_in[l]
        hy_in, q, k, v, gates = jnp.split(proj, [s_hy, s_q, s_k, s_v], axis=-1)

        hy_in = short_conv_centred(hy_in, hy_conv_w[l], hy_conv_b[l])
        hv, hx1, hx2 = jnp.split(hy_in, HY_ORDER + 1, axis=-1)
        filt = hyena_filters(L, hy_filt_w1[l], hy_filt_b1[l], hy_filt_w2[l], hy_filt_b2[l],
                             hy_filt_w3[l], hy_filt_b3[l], hy_filt_freq[l], hy_filt_w_out[l])
        z = hx1 * bidir_long_conv(hv, filt[:, 0, 0], filt[:, 0, 1], hy_bias[l, 0])
        y_hy = hx2 * bidir_long_conv(z, filt[:, 1, 0], filt[:, 1, 1], hy_bias[l, 1])

        lam = (jnp.exp(jnp.sum(lam_q1[l].astype(jnp.float32) * lam_k1[l]))
               - jnp.exp(jnp.sum(lam_q2[l].astype(jnp.float32) * lam_k2[l])) + lam_init)
        y_att = diff_attention(q.reshape(B, L, ATT_HEADS, 2, ATT_HEAD_DIM),
                               k.reshape(B, L, ATT_HEADS, 2, ATT_HEAD_DIM),
                               v.reshape(B, L, ATT_HEADS, ATT_V_DIM),
                               positions, q_norm_g[l], k_norm_g[l], lam, subln_g[l], lam_init)

        g_hy, g_att = jnp.split(jax.nn.sigmoid(gates), N_BRANCHES, axis=-1)
        merged = g_hy * (y_hy @ w_proj_hy[l]) + g_att * (y_att @ w_proj_att[l])
        x = x + g1 * (merged @ w_out[l])

        h2 = rmsnorm(x, norm2_g[l]) * (1.0 + sc2) + sh2
        f = jax.nn.silu(h2 @ w_gate[l]) * (h2 @ w_up[l])
        x = x + g2 * (f @ w_down[l])
    return x
```

```python
import functools
import math

import numpy as np
import jax
import jax.numpy as jnp
from jax import lax
from jax.experimental import pallas as pl
from jax.experimental.pallas import tpu as pltpu

F32 = jnp.float32
BF16 = jnp.bfloat16

ATT_HEADS = 8
ATT_HEAD_DIM = 64
ATT_V_DIM = 2 * ATT_HEAD_DIM
ROPE_DIM = ATT_HEAD_DIM // 4
ROPE_THETA = 500000.0
HY_WIDTH = 1024
HY_ORDER = 2
HY_EMB_BANDS = 16
HY_DECAY_TARGET = 1e-2
HY_MIN_DECAY = math.log(HY_DECAY_TARGET) / 0.3
HY_MAX_DECAY = math.log(HY_DECAY_TARGET) / 1.5
N_MOD = 6
EPS = 1e-6

LANES = 128
SUBLANES = 8
VMEM_LIMIT_BYTES = 56 * 1024 * 1024

DFT_P = LANES


def _cparams(*sem):
    return pltpu.CompilerParams(dimension_semantics=sem, vmem_limit_bytes=VMEM_LIMIT_BYTES)


def _resident(block_shape, index_map):
    return pl.BlockSpec(block_shape, index_map, pipeline_mode=pl.Buffered(1))


def _ada_kernel(ct_ref, w_ref, b_ref, o_ref):
    w = w_ref[...]
    for b in range(ct_ref.shape[0]):
        cb = ct_ref[b]
        s = cb * jax.nn.sigmoid(cb)
        o_ref[b] = jnp.sum(w * s, axis=0, keepdims=True) + b_ref[...]


def _ada_mod(c, w_ada, b_ada):
    B, D = c.shape
    n = w_ada.shape[1]
    tn = 512
    out = pl.pallas_call(
        _ada_kernel,
        grid=(n // tn,),
        in_specs=[
            pl.BlockSpec((B, D, 1), lambda j: (0, 0, 0)),
            pl.BlockSpec((D, tn), lambda j: (0, j)),
            pl.BlockSpec((1, tn), lambda j: (0, j)),
        ],
        out_specs=pl.BlockSpec((B, 1, tn), lambda j: (0, 0, j)),
        out_shape=jax.ShapeDtypeStruct((B, 1, n), F32),
        compiler_params=_cparams("arbitrary"),
        name="ada_mod",
    )(c.reshape(B, D, 1), w_ada, b_ada.reshape(1, n))
    return out.reshape(B, N_MOD, D)


def _modulated_norm(x, g, shift, scale):
    ms = jnp.mean(x * x, axis=-1, keepdims=True)
    return (x * lax.rsqrt(ms + EPS) * g) * (1.0 + scale) + shift


def _inproj_kernel(x_ref, mod_ref, g_ref, w_ref, pos_ref, qg_ref, kg_ref, inv_ref, bd_ref,
                   o_ref, h_scr, *, jq, jk, jgate):
    j = pl.program_id(1)

    @pl.when(j == 0)
    def _():
        h = _modulated_norm(x_ref[...], g_ref[...], mod_ref[0, 0:1, :], mod_ref[0, 1:2, :])
        h_scr[...] = h.astype(BF16)

    acc = jnp.dot(h_scr[...], w_ref[...], preferred_element_type=F32)

    def qk_norm_rope(gain, out_scale):
        tn = acc.shape[1]
        sq = (acc * acc).astype(BF16)
        ms = jnp.concatenate(
            [jnp.dot(sq[:, g * LANES:(g + 1) * LANES], bd_ref[...], preferred_element_type=F32)
             for g in range(tn // LANES)], axis=1)
        y = acc * lax.rsqrt(ms + EPS) * gain
        ang = pos_ref[...] * inv_ref[...]
        cos, sin = jnp.cos(ang), jnp.sin(ang)
        lane = lax.broadcasted_iota(jnp.int32, (1, LANES), 1) % ATT_HEAD_DIM
        half = ROPE_DIM // 2
        c_t = jnp.where(lane < ROPE_DIM, cos, 1.0)
        s_lo = jnp.where(lane < half, -sin, 0.0)
        s_hi = jnp.where((lane >= half) & (lane < ROPE_DIM), sin, 0.0)
        reps = tn // LANES
        c_t, s_lo, s_hi = (jnp.tile(t, (1, reps)) for t in (c_t, s_lo, s_hi))
        y_next = pltpu.roll(y, tn - half, axis=1)
        y_prev = pltpu.roll(y, half, axis=1)
        return (y * c_t + y_next * s_lo + y_prev * s_hi) * out_scale

    @pl.when(j == jq)
    def _():
        o_ref[...] = qk_norm_rope(qg_ref[...], ATT_HEAD_DIM ** -0.5).astype(o_ref.dtype)

    @pl.when(j == jk)
    def _():
        o_ref[...] = qk_norm_rope(kg_ref[...], 1.0).astype(o_ref.dtype)

    @pl.when(j >= jgate)
    def _():
        o_ref[...] = jax.nn.sigmoid(acc).astype(o_ref.dtype)

    @pl.when((j != jq) & (j != jk) & (j < jgate))
    def _():
        o_ref[...] = acc.astype(o_ref.dtype)


def _in_proj(x2, mod, norm_g, w_in_bf, pos_col, q_g, k_g, seq_len):
    T, D = x2.shape
    n = w_in_bf.shape[1]
    tm, tn = 512, 1024
    blocks_per_seq = seq_len // tm
    s_hy = (HY_ORDER + 1) * HY_WIDTH
    qk_w = ATT_HEADS * 2 * ATT_HEAD_DIM
    assert tn == qk_w and s_hy % tn == 0
    jq = s_hy // tn
    jk = jq + 1
    jgate = jk + 2
    half = ROPE_DIM // 2
    inv = ROPE_THETA ** (-jnp.arange(half, dtype=F32) * (2.0 / ROPE_DIM))
    lane = np.arange(LANES) % ATT_HEAD_DIM
    inv_lane = jnp.where(lane < ROPE_DIM, inv[lane % half], 0.0).reshape(1, LANES).astype(F32)
    grp = np.arange(LANES) // ATT_HEAD_DIM
    bd = jnp.asarray((grp[:, None] == grp[None, :]).astype(np.float32) / ATT_HEAD_DIM, BF16)
    tile_g = lambda g: jnp.tile(g.reshape(1, ATT_HEAD_DIM), (1, tn // ATT_HEAD_DIM))
    kern = functools.partial(_inproj_kernel, jq=jq, jk=jk, jgate=jgate)
    return pl.pallas_call(
        kern,
        grid=(T // tm, n // tn),
        in_specs=[
            pl.BlockSpec((tm, D), lambda i, j: (i, 0)),
            pl.BlockSpec((1, N_MOD, D), lambda i, j: (i // blocks_per_seq, 0, 0)),
            pl.BlockSpec((1, D), lambda i, j: (0, 0)),
            pl.BlockSpec((D, tn), lambda i, j: (0, j)),
            pl.BlockSpec((tm, 1), lambda i, j: (i, 0)),
            pl.BlockSpec((1, tn), lambda i, j: (0, 0)),
            pl.BlockSpec((1, tn), lambda i, j: (0, 0)),
            pl.BlockSpec((1, LANES), lambda i, j: (0, 0)),
            pl.BlockSpec((LANES, LANES), lambda i, j: (0, 0)),
        ],
        out_specs=pl.BlockSpec((tm, tn), lambda i, j: (i, j)),
        out_shape=jax.ShapeDtypeStruct((T, n), BF16),
        scratch_shapes=[pltpu.VMEM((tm, D), BF16)],
        compiler_params=_cparams("parallel", "arbitrary"),
        name="in_proj",
    )(x2, mod, norm_g.reshape(1, D), w_in_bf, pos_col, tile_g(q_g), tile_g(k_g), inv_lane, bd)


def _filter_kernel(z_ref, w1_ref, b1_ref, w2_ref, b2_ref, w3_ref, b3_ref, fr_ref, wo_ref, dl_ref,
                   o_ref, *, n_chunks):
    tl = z_ref.shape[0]
    z = z_ref[...]
    h = jnp.sin(fr_ref[0:1, :] * (jnp.dot(z, w1_ref[...], preferred_element_type=F32) + b1_ref[...]))
    h = jnp.sin(fr_ref[1:2, :] * (jnp.dot(h, w2_ref[...], preferred_element_type=F32) + b2_ref[...]))
    h = jnp.sin(fr_ref[2:3, :] * (jnp.dot(h, w3_ref[...], preferred_element_type=F32) + b3_ref[...]))
    t = z[:, 0:1]
    decay = jnp.exp(-t * dl_ref[...])
    row = pl.program_id(0) * tl + lax.broadcasted_iota(jnp.int32, (tl, 1), 0)
    not_first = (row != 0).astype(F32)
    width = dl_ref.shape[1]
    for ci in range(n_chunks):
        f = jnp.dot(h, wo_ref[:, ci * width:(ci + 1) * width], preferred_element_type=F32) * decay
        if ci % 2 == 1:
            f = f * not_first
        o_ref[ci] = f


def _hyena_filters(L, w1, b1, w2, b2, w3, b3, freq, w_out):
    emb = w1.shape[0]
    hid = w1.shape[1]
    t = jnp.linspace(0.0, 1.0, L, dtype=F32)[:, None]
    w = (2.0 * math.pi / L) * jnp.arange(L, dtype=F32)[:, None]
    bands = jnp.linspace(1e-4, HY_EMB_BANDS - 1, HY_EMB_BANDS, dtype=F32)
    z = jnp.concatenate([t, jnp.cos(bands * w), -jnp.sin(bands * w)], axis=-1)
    z = jnp.pad(z, ((0, 0), (0, LANES - emb)))
    w1p = jnp.pad(w1, ((0, LANES - emb), (0, 0)))
    deltas = jnp.abs(jnp.linspace(HY_MIN_DECAY, HY_MAX_DECAY, HY_WIDTH, dtype=F32)).reshape(1, HY_WIDTH)
    n_chunks = w_out.shape[1] // HY_WIDTH
    tl = min(512, L)
    full = lambda a: pl.BlockSpec(a.shape, lambda i: (0,) * a.ndim)
    b1, b2, b3 = (b.reshape(1, hid) for b in (b1, b2, b3))
    args = (z, w1p, b1, w2, b2, w3, b3, freq, w_out, deltas)
    return pl.pallas_call(
        functools.partial(_filter_kernel, n_chunks=n_chunks),
        grid=(L // tl,),
        in_specs=[pl.BlockSpec((tl, LANES), lambda i: (i, 0))] + [full(a) for a in args[1:]],
        out_specs=pl.BlockSpec((n_chunks, tl, HY_WIDTH), lambda i: (0, i, 0)),
        out_shape=jax.ShapeDtypeStruct((n_chunks, L, HY_WIDTH), F32),
        compiler_params=_cparams("parallel"),
        name="hy_filter",
    )(*args)


def _dft_tables(L):
    P = DFT_P
    R = L // P
    Q = 2 * R
    N = 2 * L
    nk1 = Q // 2 + 1
    mr = -(-2 * nk1 // SUBLANES) * SUBLANES
    n2 = np.arange(P)[:, None, None]
    k1 = np.arange(nk1)[None, :, None]
    n1 = np.arange(R)[None, None, :]
    ph = 2.0 * np.pi * ((k1 * (P * n1 + n2)) % N) / N
    f1 = np.zeros((P, mr, R))
    f1[:, 0:2 * nk1:2, :] = np.cos(ph)
    f1[:, 1:2 * nk1:2, :] = -np.sin(ph)
    ck = np.where((np.arange(nk1) == 0) | (np.arange(nk1) == Q // 2), 1.0, 2.0)[None, :, None] / N
    g = np.zeros((P, R, mr))
    g[:, :, 0:2 * nk1:2] = np.transpose(ck * np.cos(ph), (0, 2, 1))
    g[:, :, 1:2 * nk1:2] = np.transpose(-ck * np.sin(ph), (0, 2, 1))
    a = np.arange(P)
    th = 2.0 * np.pi * ((a[:, None] * a[None, :]) % P) / P
    tr, ti = np.cos(th), -np.sin(th)
    d_fwd = np.block([[tr, -ti], [ti, tr]])
    d_inv = np.block([[tr, ti], [-ti, tr]])
    as_bf = lambda m: jnp.asarray(m, dtype=F32).astype(BF16)
    return dict(f1=as_bf(f1), g=as_bf(g), d_fwd=as_bf(d_fwd), d_inv=as_bf(d_inv), nk1=nk1, mr=mr, R=R)


def _fwd_dft(u_scr, a_scr, f1_ref, dfwd_ref, nk1, R, post):
    P = DFT_P
    mr = f1_ref.shape[1]

    def step1(n2, _):
        xs = u_scr[pl.ds(n2, R, stride=P), :].astype(BF16)
        a_scr[pl.ds(n2, mr, stride=P), :] = jnp.dot(f1_ref[n2], xs, preferred_element_type=F32)
        return 0

    lax.fori_loop(0, P, step1, 0)

    def step3(k1, _):
        a = a_scr[pl.ds(pl.multiple_of(k1 * 2 * P, 2 * P), 2 * P), :].astype(BF16)
        post(k1, jnp.dot(dfwd_ref[...], a, preferred_element_type=F32))
        return 0

    lax.fori_loop(0, nk1, step3, 0)


def _filtfft_kernel(hf_ref, hb_ref, f1_ref, dfwd_ref, o_ref, a_scr, s_scr, *, nk1, R):
    P = DFT_P

    def put(k1, x):
        s_scr[k1] = x

    def add_conj(k1, x):
        s = s_scr[k1]
        o_ref[0, k1] = jnp.concatenate([s[0:P] + x[0:P], s[P:2 * P] - x[P:2 * P]], axis=0).astype(o_ref.dtype)

    _fwd_dft(hf_ref.at[0], a_scr, f1_ref, dfwd_ref, nk1, R, put)
    _fwd_dft(hb_ref.at[0], a_scr, f1_ref, dfwd_ref, nk1, R, add_conj)


def _filter_spectra(filt, tabs):
    n_f, L, C = filt.shape
    P = DFT_P
    nk1, mr, R = tabs["nk1"], tabs["mr"], tabs["R"]
    kern = functools.partial(_filtfft_kernel, nk1=nk1, R=R)
    return pl.pallas_call(
        kern,
        grid=(n_f // 2, C // LANES),
        in_specs=[
            pl.BlockSpec((1, L, LANES), lambda o, c: (2 * o, 0, c)),
            pl.BlockSpec((1, L, LANES), lambda o, c: (2 * o + 1, 0, c)),
            _resident((P, mr, R), lambda o, c: (0, 0, 0)),
            _resident((2 * P, 2 * P), lambda o, c: (0, 0)),
        ],
        out_specs=pl.BlockSpec((1, nk1, 2 * P, LANES), lambda o, c: (o, 0, 0, c)),
        out_shape=jax.ShapeDtypeStruct((n_f // 2, nk1, 2 * P, C), BF16),
        scratch_shapes=[pltpu.VMEM((mr * P, LANES), F32), pltpu.VMEM((nk1, 2 * P, LANES), F32)],
        compiler_params=_cparams("parallel", "parallel"),
        name="hy_filtfft",
    )(filt, filt, tabs["f1"], tabs["d_fwd"])


def _short_conv_chunk(u_ref, r0, rows, L, w_ref, b_ref):
    halo = 16
    c = u_ref[pl.ds(r0, rows), :].astype(F32)
    p0 = pl.multiple_of(jnp.maximum(r0 - halo, 0), halo)
    n0 = pl.multiple_of(jnp.minimum(r0 + rows, L - halo), halo)
    prev = u_ref[pl.ds(p0, halo), :].astype(F32)[halo - 1:halo]
    nxt = u_ref[pl.ds(n0, halo), :].astype(F32)[0:1]
    prev = jnp.where(r0 > 0, prev, 0.0)
    nxt = jnp.where(r0 + rows < L, nxt, 0.0)
    row = lax.broadcasted_iota(jnp.int32, (rows, 1), 0)
    up = jnp.where(row == 0, prev, pltpu.roll(c, 1, axis=0))
    dn = jnp.where(row == rows - 1, nxt, pltpu.roll(c, rows - 1, axis=0))
    return up * w_ref[0:1, :] + c * w_ref[1:2, :] + dn * w_ref[2:3, :] + b_ref[...]


def _hyconv_kernel(u_ref, gt_ref, kf_ref, f1_ref, g_ref, dfwd_ref, dinv_ref, wu_ref, bu_ref,
                   wg_ref, bg_ref, bias_ref, o_ref, u_scr, a_scr, y_scr, *, nk1, R, conv_u, chunk):
    P = DFT_P
    L = u_scr.shape[0]
    n_chunks = L // chunk
    mr = f1_ref.shape[1]

    def load_u(i, _):
        r0 = pl.multiple_of(i * chunk, chunk)
        if conv_u:
            u_scr[pl.ds(r0, chunk), :] = _short_conv_chunk(u_ref, r0, chunk, L, wu_ref, bu_ref)
        else:
            u_scr[pl.ds(r0, chunk), :] = u_ref[pl.ds(r0, chunk), :].astype(F32)
        return 0

    lax.fori_loop(0, n_chunks, load_u, 0)

    def spectral(k1, x):
        kf = kf_ref[0, k1].astype(F32)
        xr, xi = x[0:P], x[P:2 * P]
        kr, ki = kf[0:P], kf[P:2 * P]
        y = jnp.concatenate([xr * kr - xi * ki, xr * ki + xi * kr], axis=0).astype(BF16)
        a_scr[pl.ds(pl.multiple_of(k1 * 2 * P, 2 * P), 2 * P), :] = jnp.dot(
            dinv_ref[...], y, preferred_element_type=F32)

    _fwd_dft(u_scr, a_scr, f1_ref, dfwd_ref, nk1, R, spectral)

    def inv_outer(n2, _):
        bs = a_scr[pl.ds(n2, mr, stride=P), :].astype(BF16)
        y_scr[pl.ds(n2, R, stride=P), :] = jnp.dot(g_ref[n2], bs, preferred_element_type=F32)
        return 0

    lax.fori_loop(0, P, inv_outer, 0)

    def epilogue(i, _):
        r0 = pl.multiple_of(i * chunk, chunk)
        gate = _short_conv_chunk(gt_ref, r0, chunk, L, wg_ref, bg_ref)
        y = y_scr[pl.ds(r0, chunk), :] + u_scr[pl.ds(r0, chunk), :] * bias_ref[...]
        o_ref[pl.ds(r0, chunk), :] = (gate * y).astype(o_ref.dtype)
        return 0

    lax.fori_loop(0, n_chunks, epilogue, 0)


def _hyena_conv(u_arr, u_col0, gate_arr, gate_col0, kf, order, conv_w, conv_b, bias, tabs, B, L, conv_u):
    P = DFT_P
    nk1, mr, R = tabs["nk1"], tabs["mr"], tabs["R"]
    C = HY_WIDTH
    n_ct = C // LANES
    chunk = min(512, L)
    kern = functools.partial(_hyconv_kernel, nk1=nk1, R=R, conv_u=conv_u, chunk=chunk)
    ucb, gcb = u_col0 // LANES, gate_col0 // LANES
    return pl.pallas_call(
        kern,
        grid=(n_ct, B),
        in_specs=[
            pl.BlockSpec((L, LANES), lambda c, b: (b, ucb + c)),
            pl.BlockSpec((L, LANES), lambda c, b: (b, gcb + c)),
            pl.BlockSpec((1, nk1, 2 * P, LANES), lambda c, b: (order, 0, 0, c)),
            _resident((P, mr, R), lambda c, b: (0, 0, 0)),
            _resident((P, R, mr), lambda c, b: (0, 0, 0)),
            _resident((2 * P, 2 * P), lambda c, b: (0, 0)),
            _resident((2 * P, 2 * P), lambda c, b: (0, 0)),
            pl.BlockSpec((3, LANES), lambda c, b: (0, ucb + c if conv_u else c)),
            pl.BlockSpec((1, LANES), lambda c, b: (0, ucb + c if conv_u else c)),
            pl.BlockSpec((3, LANES), lambda c, b: (0, gcb + c)),
            pl.BlockSpec((1, LANES), lambda c, b: (0, gcb + c)),
            pl.BlockSpec((1, LANES), lambda c, b: (0, c)),
        ],
        out_specs=pl.BlockSpec((L, LANES), lambda c, b: (b, c)),
        out_shape=jax.ShapeDtypeStruct((B * L, C), BF16),
        scratch_shapes=[
            pltpu.VMEM((L, LANES), F32),
            pltpu.VMEM((mr * P, LANES), F32),
            pltpu.VMEM((L, LANES), F32),
        ],
        compiler_params=_cparams("parallel", "arbitrary"),
        name=f"hy_conv{order}",
    )(u_arr, gate_arr, kf, tabs["f1"], tabs["g"], tabs["d_fwd"], tabs["d_inv"],
      conv_w, conv_b, conv_w, conv_b, bias[order].reshape(1, C))


def _attn_kernel(q_ref, k_ref, v_ref, lamv_ref, sg_ref, o_ref, *, tk, lam_init):
    tq = q_ref.shape[0]
    L = k_ref.shape[0]
    d = ATT_HEAD_DIM
    q = q_ref[...]
    lane = lax.broadcasted_iota(jnp.int32, (1, 2 * d), 1)
    qs = (jnp.where(lane < d, q, jnp.zeros_like(q)), jnp.where(lane >= d, q, jnp.zeros_like(q)))
    nt = (((1,), (1,)), ((), ()))

    def body(i, carry):
        r0 = pl.multiple_of(i * tk, tk)
        kc = k_ref[pl.ds(r0, tk), :]
        vc = v_ref[pl.ds(r0, tk), :]
        out = []
        for comp in range(2):
            m, l, acc = carry[comp]
            s = lax.dot_general(qs[comp], kc, nt, preferred_element_type=F32)
            m_new = jnp.maximum(m, jnp.max(s, axis=-1, keepdims=True))
            alpha = jnp.exp(m - m_new)
            p = jnp.exp(s - m_new)
            l_new = alpha * l + jnp.sum(p, axis=-1, keepdims=True)
            acc_new = alpha * acc + jnp.dot(p.astype(BF16), vc, preferred_element_type=F32)
            out.append((m_new, l_new, acc_new))
        return tuple(out)

    init = tuple((jnp.full((tq, 1), -jnp.inf, F32), jnp.zeros((tq, 1), F32), jnp.zeros((tq, 2 * d), F32))
                 for _ in range(2))
    (m1, l1, a1), (m2, l2, a2) = lax.fori_loop(0, L // tk, body, init)

    lv = lamv_ref[...]
    lam = (jnp.exp(jnp.sum(lv[0:1] * lv[1:2], axis=-1, keepdims=True))
           - jnp.exp(jnp.sum(lv[2:3] * lv[3:4], axis=-1, keepdims=True)) + lam_init)
    o = a1 / l1 - lam * (a2 / l2)
    ms = jnp.mean(o * o, axis=-1, keepdims=True)
    o_ref[...] = (o * lax.rsqrt(ms + EPS) * sg_ref[...] * (1.0 - lam_init)).astype(o_ref.dtype)


def _diff_attention(proj, q_col0, k_col0, v_col0, lam_vecs, subln_g, lam_init, B, L):
    tq = min(512, L)
    tk = min(512, L)
    hw = ATT_V_DIM
    qcb, kcb, vcb = q_col0 // hw, k_col0 // hw, v_col0 // hw
    nq = L // tq
    kern = functools.partial(_attn_kernel, tk=tk, lam_init=lam_init)
    return pl.pallas_call(
        kern,
        grid=(B, ATT_HEADS, nq),
        in_specs=[
            pl.BlockSpec((tq, hw), lambda b, h, i: (b * nq + i, qcb + h)),
            pl.BlockSpec((L, hw), lambda b, h, i: (b, kcb + h)),
            pl.BlockSpec((L, hw), lambda b, h, i: (b, vcb + h)),
            pl.BlockSpec((4, ATT_HEAD_DIM), lambda b, h, i: (0, 0)),
            pl.BlockSpec((1, hw), lambda b, h, i: (0, 0)),
        ],
        out_specs=pl.BlockSpec((tq, hw), lambda b, h, i: (b * nq + i, h)),
        out_shape=jax.ShapeDtypeStruct((B * L, ATT_HEADS * hw), BF16),
        compiler_params=_cparams("parallel", "parallel", "arbitrary"),
        name="diff_attn",
    )(proj, proj, proj, lam_vecs, subln_g.reshape(1, hw))


def _merge_kernel(yh_ref, ya_ref, gh_ref, ga_ref, wph_ref, wpa_ref, wo_ref, x_ref, mod_ref,
                  o_ref, m_scr):
    j = pl.program_id(1)

    @pl.when(j == 0)
    def _():
        mh = jnp.dot(yh_ref[...], wph_ref[...], preferred_element_type=F32)
        ma = jnp.dot(ya_ref[...], wpa_ref[...], preferred_element_type=F32)
        m_scr[...] = (gh_ref[...].astype(F32) * mh + ga_ref[...].astype(F32) * ma).astype(BF16)

    r = jnp.dot(m_scr[...], wo_ref[...], preferred_element_type=F32)
    o_ref[...] = x_ref[...] + mod_ref[0, 2:3, :] * r


def _merge_out(y_hy, y_att, proj, gate_col0, wph, wpa, wo, x2, mod, seq_len):
    T, D = x2.shape
    tm, tn = 512, 512
    bps = seq_len // tm
    gcb = gate_col0 // D
    cw = y_hy.shape[1]
    return pl.pallas_call(
        _merge_kernel,
        grid=(T // tm, D // tn),
        in_specs=[
            pl.BlockSpec((tm, cw), lambda i, j: (i, 0)),
            pl.BlockSpec((tm, cw), lambda i, j: (i, 0)),
            pl.BlockSpec((tm, D), lambda i, j: (i, gcb)),
            pl.BlockSpec((tm, D), lambda i, j: (i, gcb + 1)),
            _resident((cw, D), lambda i, j: (0, 0)),
            _resident((cw, D), lambda i, j: (0, 0)),
            pl.BlockSpec((D, tn), lambda i, j: (0, j)),
            pl.BlockSpec((tm, tn), lambda i, j: (i, j)),
            pl.BlockSpec((1, N_MOD, tn), lambda i, j: (i // bps, 0, j)),
        ],
        out_specs=pl.BlockSpec((tm, tn), lambda i, j: (i, j)),
        out_shape=jax.ShapeDtypeStruct((T, D), F32),
        scratch_shapes=[pltpu.VMEM((tm, D), BF16)],
        compiler_params=_cparams("parallel", "arbitrary"),
        name="merge_out",
    )(y_hy, y_att, proj, proj, wph, wpa, wo, x2, mod)


def _ffn_kernel(x_ref, mod_ref, g_ref, wg_ref, wu_ref, wd_ref, o_ref, h_scr, acc_scr):
    j = pl.program_id(1)

    @pl.when(j == 0)
    def _():
        h = _modulated_norm(x_ref[...], g_ref[...], mod_ref[0, 3:4, :], mod_ref[0, 4:5, :])
        h_scr[...] = h.astype(BF16)
        acc_scr[...] = jnp.zeros_like(acc_scr)

    h = h_scr[...]
    a = jnp.dot(h, wg_ref[...], preferred_element_type=F32)
    u = jnp.dot(h, wu_ref[...], preferred_element_type=F32)
    f = (a * jax.nn.sigmoid(a) * u).astype(BF16)
    acc_scr[...] += jnp.dot(f, wd_ref[...], preferred_element_type=F32)

    @pl.when(j == pl.num_programs(1) - 1)
    def _():
        o_ref[...] = x_ref[...] + mod_ref[0, 5:6, :] * acc_scr[...]


def _ffn(x2, mod, norm_g, wg, wu, wd, seq_len):
    T, D = x2.shape
    H = wg.shape[1]
    tm, th = 512, 512
    bps = seq_len // tm
    return pl.pallas_call(
        _ffn_kernel,
        grid=(T // tm, H // th),
        in_specs=[
            pl.BlockSpec((tm, D), lambda i, j: (i, 0)),
            pl.BlockSpec((1, N_MOD, D), lambda i, j: (i // bps, 0, 0)),
            pl.BlockSpec((1, D), lambda i, j: (0, 0)),
            pl.BlockSpec((D, th), lambda i, j: (0, j)),
            pl.BlockSpec((D, th), lambda i, j: (0, j)),
            pl.BlockSpec((th, D), lambda i, j: (j, 0)),
        ],
        out_specs=pl.BlockSpec((tm, D), lambda i, j: (i, 0)),
        out_shape=jax.ShapeDtypeStruct((T, D), F32),
        scratch_shapes=[pltpu.VMEM((tm, D), BF16), pltpu.VMEM((tm, D), F32)],
        compiler_params=_cparams("parallel", "arbitrary"),
        name="ffn",
    )(x2, mod, norm_g.reshape(1, D), wg, wu, wd)


def kernel(x, c, positions, w_ada, b_ada, norm1_g, w_in, hy_conv_w, hy_conv_b, hy_filt_w1, hy_filt_b1, hy_filt_w2, hy_filt_b2, hy_filt_w3, hy_filt_b3, hy_filt_freq, hy_filt_w_out, hy_bias, q_norm_g, k_norm_g, lam_q1, lam_k1, lam_q2, lam_k2, subln_g, w_proj_hy, w_proj_att, w_out, norm2_g, w_gate, w_up, w_down):
    B, L, D = x.shape
    T = B * L
    depth = w_ada.shape[0]
    s_hy = (HY_ORDER + 1) * HY_WIDTH
    qk_w = ATT_HEADS * 2 * ATT_HEAD_DIM
    q0, k0 = s_hy, s_hy + qk_w
    v0 = k0 + qk_w
    g0 = v0 + ATT_HEADS * ATT_V_DIM
    tabs = _dft_tables(L)
    pos_col = positions.astype(F32).reshape(T, 1)
    x2 = x.reshape(T, D)
    for l in range(depth):
        lam_init = 0.8 - 0.6 * math.exp(-0.3 * l)
        mod = _ada_mod(c, w_ada[l], b_ada[l])
        proj = _in_proj(x2, mod, norm1_g[l], w_in[l].astype(BF16), pos_col,
                        q_norm_g[l], k_norm_g[l], L)

        filt = _hyena_filters(L, hy_filt_w1[l], hy_filt_b1[l], hy_filt_w2[l], hy_filt_b2[l],
                              hy_filt_w3[l], hy_filt_b3[l], hy_filt_freq[l], hy_filt_w_out[l])
        kf = _filter_spectra(filt, tabs)
        cw, cb = hy_conv_w[l], hy_conv_b[l].reshape(1, s_hy)
        z = _hyena_conv(proj, 0, proj, HY_WIDTH, kf, 0, cw, cb, hy_bias[l], tabs, B, L, True)
        y_hy = _hyena_conv(z, 0, proj, 2 * HY_WIDTH, kf, 1, cw, cb, hy_bias[l], tabs, B, L, False)

        lam_vecs = jnp.stack([lam_q1[l], lam_k1[l], lam_q2[l], lam_k2[l]]).astype(F32)
        y_att = _diff_attention(proj, q0, k0, v0, lam_vecs, subln_g[l], lam_init, B, L)

        x2 = _merge_out(y_hy, y_att, proj, g0, w_proj_hy[l].astype(BF16), w_proj_att[l].astype(BF16),
                        w_out[l].astype(BF16), x2, mod, L)
        x2 = _ffn(x2, mod, norm2_g[l], w_gate[l].astype(BF16), w_up[l].astype(BF16),
                  w_down[l].astype(BF16), L)
    return x2.reshape(B, L, D)
```

```python
import functools
import math

import numpy as np
import jax
import jax.numpy as jnp
from jax import lax
from jax.experimental import pallas as pl
from jax.experimental.pallas import tpu as pltpu

F32 = jnp.float32
BF16 = jnp.bfloat16

ATT_HEADS = 8
ATT_HEAD_DIM = 64
ATT_V_DIM = 2 * ATT_HEAD_DIM
ROPE_DIM = ATT_HEAD_DIM // 4
ROPE_THETA = 500000.0
HY_WIDTH = 1024
HY_ORDER = 2
HY_EMB_BANDS = 16
HY_DECAY_TARGET = 1e-2
HY_MIN_DECAY = math.log(HY_DECAY_TARGET) / 0.3
HY_MAX_DECAY = math.log(HY_DECAY_TARGET) / 1.5
N_MOD = 6
EPS = 1e-6
LOG2_E = math.log2(math.e)

LANES = 128
SUBLANES = 8
VMEM_LIMIT_BYTES = 56 * 1024 * 1024

DFT_P = LANES
OUTER_UNROLL = 8


def _inner_unroll(nk1):
    return next((u for u in (5, 4, 3, 2) if nk1 % u == 0), 1)


def _cparams(*sem):
    return pltpu.CompilerParams(dimension_semantics=sem, vmem_limit_bytes=VMEM_LIMIT_BYTES)


def _resident(block_shape, index_map):
    return pl.BlockSpec(block_shape, index_map, pipeline_mode=pl.Buffered(1))


def _ada_kernel(ct_ref, w_ref, b_ref, o_ref):
    w = w_ref[...]
    for b in range(ct_ref.shape[0]):
        cb = ct_ref[b]
        s = cb * jax.nn.sigmoid(cb)
        o_ref[b] = jnp.sum(w * s, axis=0, keepdims=True) + b_ref[...]


def _ada_mod(c, w_ada, b_ada):
    B, D = c.shape
    n = w_ada.shape[1]
    tn = 512
    out = pl.pallas_call(
        _ada_kernel,
        grid=(n // tn,),
        in_specs=[
            pl.BlockSpec((B, D, 1), lambda j: (0, 0, 0)),
            pl.BlockSpec((D, tn), lambda j: (0, j)),
            pl.BlockSpec((1, tn), lambda j: (0, j)),
        ],
        out_specs=pl.BlockSpec((B, 1, tn), lambda j: (0, 0, j)),
        out_shape=jax.ShapeDtypeStruct((B, 1, n), F32),
        compiler_params=_cparams("arbitrary"),
        name="ada_mod",
    )(c.reshape(B, D, 1), w_ada, b_ada.reshape(1, n))
    return out.reshape(B, N_MOD, D)


def _modulated_norm(x, g, shift, scale):
    ms = jnp.mean(x * x, axis=-1, keepdims=True)
    return (x * lax.rsqrt(ms + EPS) * g) * (1.0 + scale) + shift


def _inproj_kernel(x_ref, mod_ref, g_ref, w_ref, pos_ref, qg_ref, kg_ref, inv_ref, bd_ref,
                   o_ref, h_scr, *, jq, jk, jgate):
    j = pl.program_id(1)

    @pl.when(j == 0)
    def _():
        h = _modulated_norm(x_ref[...], g_ref[...], mod_ref[0, 0:1, :], mod_ref[0, 1:2, :])
        h_scr[...] = h.astype(BF16)

    acc = jnp.dot(h_scr[...], w_ref[...], preferred_element_type=F32)

    def qk_norm_rope(gain, out_scale):
        tn = acc.shape[1]
        sq = (acc * acc).astype(BF16)
        ms = jnp.concatenate(
            [jnp.dot(sq[:, g * LANES:(g + 1) * LANES], bd_ref[...], preferred_element_type=F32)
             for g in range(tn // LANES)], axis=1)
        y = acc * lax.rsqrt(ms + EPS) * gain
        ang = pos_ref[...] * inv_ref[...]
        cos, sin = jnp.cos(ang), jnp.sin(ang)
        lane = lax.broadcasted_iota(jnp.int32, (1, LANES), 1) % ATT_HEAD_DIM
        half = ROPE_DIM // 2
        c_t = jnp.where(lane < ROPE_DIM, cos, 1.0)
        s_lo = jnp.where(lane < half, -sin, 0.0)
        s_hi = jnp.where((lane >= half) & (lane < ROPE_DIM), sin, 0.0)
        reps = tn // LANES
        c_t, s_lo, s_hi = (jnp.tile(t, (1, reps)) for t in (c_t, s_lo, s_hi))
        y_next = pltpu.roll(y, tn - half, axis=1)
        y_prev = pltpu.roll(y, half, axis=1)
        return (y * c_t + y_next * s_lo + y_prev * s_hi) * out_scale

    @pl.when(j == jq)
    def _():
        o_ref[...] = qk_norm_rope(qg_ref[...], ATT_HEAD_DIM ** -0.5 * LOG2_E).astype(o_ref.dtype)

    @pl.when(j == jk)
    def _():
        o_ref[...] = qk_norm_rope(kg_ref[...], 1.0).astype(o_ref.dtype)

    @pl.when(j >= jgate)
    def _():
        o_ref[...] = jax.nn.sigmoid(acc).astype(o_ref.dtype)

    @pl.when((j != jq) & (j != jk) & (j < jgate))
    def _():
        o_ref[...] = acc.astype(o_ref.dtype)


def _in_proj(x2, mod, norm_g, w_in_bf, pos_col, q_g, k_g, seq_len):
    T, D = x2.shape
    n = w_in_bf.shape[1]
    tm, tn = 512, 1024
    blocks_per_seq = seq_len // tm
    s_hy = (HY_ORDER + 1) * HY_WIDTH
    qk_w = ATT_HEADS * 2 * ATT_HEAD_DIM
    assert tn == qk_w and s_hy % tn == 0
    jq = s_hy // tn
    jk = jq + 1
    jgate = jk + 2
    half = ROPE_DIM // 2
    inv = ROPE_THETA ** (-jnp.arange(half, dtype=F32) * (2.0 / ROPE_DIM))
    lane = np.arange(LANES) % ATT_HEAD_DIM
    inv_lane = jnp.where(lane < ROPE_DIM, inv[lane % half], 0.0).reshape(1, LANES).astype(F32)
    grp = np.arange(LANES) // ATT_HEAD_DIM
    bd = jnp.asarray((grp[:, None] == grp[None, :]).astype(np.float32) / ATT_HEAD_DIM, BF16)
    tile_g = lambda g: jnp.tile(g.reshape(1, ATT_HEAD_DIM), (1, tn // ATT_HEAD_DIM))
    kern = functools.partial(_inproj_kernel, jq=jq, jk=jk, jgate=jgate)
    return pl.pallas_call(
        kern,
        grid=(T // tm, n // tn),
        in_specs=[
            pl.BlockSpec((tm, D), lambda i, j: (i, 0)),
            pl.BlockSpec((1, N_MOD, D), lambda i, j: (i // blocks_per_seq, 0, 0)),
            pl.BlockSpec((1, D), lambda i, j: (0, 0)),
            pl.BlockSpec((D, tn), lambda i, j: (0, j)),
            pl.BlockSpec((tm, 1), lambda i, j: (i, 0)),
            pl.BlockSpec((1, tn), lambda i, j: (0, 0)),
            pl.BlockSpec((1, tn), lambda i, j: (0, 0)),
            pl.BlockSpec((1, LANES), lambda i, j: (0, 0)),
            pl.BlockSpec((LANES, LANES), lambda i, j: (0, 0)),
        ],
        out_specs=pl.BlockSpec((tm, tn), lambda i, j: (i, j)),
        out_shape=jax.ShapeDtypeStruct((T, n), BF16),
        scratch_shapes=[pltpu.VMEM((tm, D), BF16)],
        compiler_params=_cparams("parallel", "arbitrary"),
        name="in_proj",
    )(x2, mod, norm_g.reshape(1, D), w_in_bf, pos_col, tile_g(q_g), tile_g(k_g), inv_lane, bd)


def _filter_kernel(z_ref, w1_ref, b1_ref, w2_ref, b2_ref, w3_ref, b3_ref, fr_ref, wo_ref, dl_ref,
                   o_ref, *, n_chunks):
    tl = z_ref.shape[0]
    z = z_ref[...]
    h = jnp.sin(fr_ref[0:1, :] * (jnp.dot(z, w1_ref[...], preferred_element_type=F32) + b1_ref[...]))
    h = jnp.sin(fr_ref[1:2, :] * (jnp.dot(h, w2_ref[...], preferred_element_type=F32) + b2_ref[...]))
    h = jnp.sin(fr_ref[2:3, :] * (jnp.dot(h, w3_ref[...], preferred_element_type=F32) + b3_ref[...]))
    t = z[:, 0:1]
    decay = jnp.exp(-t * dl_ref[...])
    row = pl.program_id(0) * tl + lax.broadcasted_iota(jnp.int32, (tl, 1), 0)
    not_first = (row != 0).astype(F32)
    width = dl_ref.shape[1]
    for ci in range(n_chunks):
        f = jnp.dot(h, wo_ref[:, ci * width:(ci + 1) * width], preferred_element_type=F32) * decay
        if ci % 2 == 1:
            f = f * not_first
        o_ref[ci] = f


def _hyena_filters(L, w1, b1, w2, b2, w3, b3, freq, w_out):
    emb = w1.shape[0]
    hid = w1.shape[1]
    t = jnp.linspace(0.0, 1.0, L, dtype=F32)[:, None]
    w = (2.0 * math.pi / L) * jnp.arange(L, dtype=F32)[:, None]
    bands = jnp.linspace(1e-4, HY_EMB_BANDS - 1, HY_EMB_BANDS, dtype=F32)
    z = jnp.concatenate([t, jnp.cos(bands * w), -jnp.sin(bands * w)], axis=-1)
    z = jnp.pad(z, ((0, 0), (0, LANES - emb)))
    w1p = jnp.pad(w1, ((0, LANES - emb), (0, 0)))
    deltas = jnp.abs(jnp.linspace(HY_MIN_DECAY, HY_MAX_DECAY, HY_WIDTH, dtype=F32)).reshape(1, HY_WIDTH)
    n_chunks = w_out.shape[1] // HY_WIDTH
    tl = min(512, L)
    full = lambda a: pl.BlockSpec(a.shape, lambda i: (0,) * a.ndim)
    b1, b2, b3 = (b.reshape(1, hid) for b in (b1, b2, b3))
    args = (z, w1p, b1, w2, b2, w3, b3, freq, w_out, deltas)
    return pl.pallas_call(
        functools.partial(_filter_kernel, n_chunks=n_chunks),
        grid=(L // tl,),
        in_specs=[pl.BlockSpec((tl, LANES), lambda i: (i, 0))] + [full(a) for a in args[1:]],
        out_specs=pl.BlockSpec((n_chunks, tl, HY_WIDTH), lambda i: (0, i, 0)),
        out_shape=jax.ShapeDtypeStruct((n_chunks, L, HY_WIDTH), F32),
        compiler_params=_cparams("parallel"),
        name="hy_filter",
    )(*args)


def _dft_tables(L):
    P = DFT_P
    R = L // P
    Q = 2 * R
    N = 2 * L
    nk1 = Q // 2 + 1
    mr = -(-2 * nk1 // SUBLANES) * SUBLANES
    n2 = np.arange(P)[:, None, None]
    k1 = np.arange(nk1)[None, :, None]
    n1 = np.arange(R)[None, None, :]
    ph = 2.0 * np.pi * ((k1 * (P * n1 + n2)) % N) / N
    f1 = np.zeros((P, mr, R))
    f1[:, 0:2 * nk1:2, :] = np.cos(ph)
    f1[:, 1:2 * nk1:2, :] = -np.sin(ph)
    ck = np.where((np.arange(nk1) == 0) | (np.arange(nk1) == Q // 2), 1.0, 2.0)[None, :, None] / N
    g = np.zeros((P, R, mr))
    g[:, :, 0:2 * nk1:2] = np.transpose(ck * np.cos(ph), (0, 2, 1))
    g[:, :, 1:2 * nk1:2] = np.transpose(-ck * np.sin(ph), (0, 2, 1))
    a = np.arange(P)
    th = 2.0 * np.pi * ((a[:, None] * a[None, :]) % P) / P
    tr, ti = np.cos(th), -np.sin(th)
    d_fwd = np.block([[tr, -ti], [ti, tr]])
    d_inv = np.block([[tr, ti], [-ti, tr]])
    as_bf = lambda m: jnp.asarray(m, dtype=F32).astype(BF16)
    return dict(f1=as_bf(f1), g=as_bf(g), d_fwd=as_bf(d_fwd), d_inv=as_bf(d_inv), nk1=nk1, mr=mr, R=R)


def _fwd_dft(u_scr, a_scr, f1_ref, dfwd_ref, nk1, R, post):
    P = DFT_P
    mr = f1_ref.shape[1]

    def step1(n2, _):
        xs = u_scr[pl.ds(n2, R, stride=P), :].astype(BF16)
        a_scr[pl.ds(n2, mr, stride=P), :] = jnp.dot(f1_ref[n2], xs, preferred_element_type=F32)
        return 0

    lax.fori_loop(0, P, step1, 0, unroll=OUTER_UNROLL)

    def step3(k1, _):
        a = a_scr[pl.ds(pl.multiple_of(k1 * 2 * P, 2 * P), 2 * P), :].astype(BF16)
        post(k1, jnp.dot(dfwd_ref[...], a, preferred_element_type=F32))
        return 0

    lax.fori_loop(0, nk1, step3, 0, unroll=_inner_unroll(nk1))


def _filtfft_kernel(hf_ref, hb_ref, f1_ref, dfwd_ref, o_ref, a_scr, s_scr, *, nk1, R):
    P = DFT_P

    def put(k1, x):
        s_scr[k1] = x

    def add_conj(k1, x):
        s = s_scr[k1]
        o_ref[0, k1] = jnp.concatenate([s[0:P] + x[0:P], s[P:2 * P] - x[P:2 * P]], axis=0).astype(o_ref.dtype)

    _fwd_dft(hf_ref.at[0], a_scr, f1_ref, dfwd_ref, nk1, R, put)
    _fwd_dft(hb_ref.at[0], a_scr, f1_ref, dfwd_ref, nk1, R, add_conj)


def _filter_spectra(filt, tabs):
    n_f, L, C = filt.shape
    P = DFT_P
    nk1, mr, R = tabs["nk1"], tabs["mr"], tabs["R"]
    kern = functools.partial(_filtfft_kernel, nk1=nk1, R=R)
    return pl.pallas_call(
        kern,
        grid=(n_f // 2, C // LANES),
        in_specs=[
            pl.BlockSpec((1, L, LANES), lambda o, c: (2 * o, 0, c)),
            pl.BlockSpec((1, L, LANES), lambda o, c: (2 * o + 1, 0, c)),
            _resident((P, mr, R), lambda o, c: (0, 0, 0)),
            _resident((2 * P, 2 * P), lambda o, c: (0, 0)),
        ],
        out_specs=pl.BlockSpec((1, nk1, 2 * P, LANES), lambda o, c: (o, 0, 0, c)),
        out_shape=jax.ShapeDtypeStruct((n_f // 2, nk1, 2 * P, C), BF16),
        scratch_shapes=[pltpu.VMEM((mr * P, LANES), F32), pltpu.VMEM((nk1, 2 * P, LANES), F32)],
        compiler_params=_cparams("parallel", "parallel"),
        name="hy_filtfft",
    )(filt, filt, tabs["f1"], tabs["d_fwd"])


def _short_conv_chunk(u_ref, r0, rows, L, w_ref, b_ref):
    halo = 16
    c = u_ref[pl.ds(r0, rows), :].astype(F32)
    p0 = pl.multiple_of(jnp.maximum(r0 - halo, 0), halo)
    n0 = pl.multiple_of(jnp.minimum(r0 + rows, L - halo), halo)
    prev = u_ref[pl.ds(p0, halo), :].astype(F32)[halo - 1:halo]
    nxt = u_ref[pl.ds(n0, halo), :].astype(F32)[0:1]
    prev = jnp.where(r0 > 0, prev, 0.0)
    nxt = jnp.where(r0 + rows < L, nxt, 0.0)
    row = lax.broadcasted_iota(jnp.int32, (rows, 1), 0)
    up = jnp.where(row == 0, prev, pltpu.roll(c, 1, axis=0))
    dn = jnp.where(row == rows - 1, nxt, pltpu.roll(c, rows - 1, axis=0))
    return up * w_ref[0:1, :] + c * w_ref[1:2, :] + dn * w_ref[2:3, :] + b_ref[...]


def _hyconv_kernel(u_ref, gt_ref, kf_ref, f1_ref, g_ref, dfwd_ref, dinv_ref, wu_ref, bu_ref,
                   wg_ref, bg_ref, bias_ref, o_ref, u_scr, a_scr, y_scr, *, nk1, R, conv_u, chunk):
    P = DFT_P
    L = u_scr.shape[0]
    n_chunks = L // chunk
    mr = f1_ref.shape[1]

    def load_u(i, _):
        r0 = pl.multiple_of(i * chunk, chunk)
        if conv_u:
            u_scr[pl.ds(r0, chunk), :] = _short_conv_chunk(u_ref, r0, chunk, L, wu_ref, bu_ref)
        else:
            u_scr[pl.ds(r0, chunk), :] = u_ref[pl.ds(r0, chunk), :].astype(F32)
        return 0

    lax.fori_loop(0, n_chunks, load_u, 0)

    def spectral(k1, x):
        kf = kf_ref[0, k1].astype(F32)
        xr, xi = x[0:P], x[P:2 * P]
        kr, ki = kf[0:P], kf[P:2 * P]
        y = jnp.concatenate([xr * kr - xi * ki, xr * ki + xi * kr], axis=0).astype(BF16)
        a_scr[pl.ds(pl.multiple_of(k1 * 2 * P, 2 * P), 2 * P), :] = jnp.dot(
            dinv_ref[...], y, preferred_element_type=F32)

    _fwd_dft(u_scr, a_scr, f1_ref, dfwd_ref, nk1, R, spectral)

    def inv_outer(n2, _):
        bs = a_scr[pl.ds(n2, mr, stride=P), :].astype(BF16)
        y_scr[pl.ds(n2, R, stride=P), :] = jnp.dot(g_ref[n2], bs, preferred_element_type=F32)
        return 0

    lax.fori_loop(0, P, inv_outer, 0, unroll=OUTER_UNROLL)

    def epilogue(i, _):
        r0 = pl.multiple_of(i * chunk, chunk)
        gate = _short_conv_chunk(gt_ref, r0, chunk, L, wg_ref, bg_ref)
        y = y_scr[pl.ds(r0, chunk), :] + u_scr[pl.ds(r0, chunk), :] * bias_ref[...]
        o_ref[pl.ds(r0, chunk), :] = (gate * y).astype(o_ref.dtype)
        return 0

    lax.fori_loop(0, n_chunks, epilogue, 0)


def _hyena_conv(u_arr, u_col0, gate_arr, gate_col0, kf, order, conv_w, conv_b, bias, tabs, B, L, conv_u):
    P = DFT_P
    nk1, mr, R = tabs["nk1"], tabs["mr"], tabs["R"]
    C = HY_WIDTH
    n_ct = C // LANES
    chunk = min(512, L)
    kern = functools.partial(_hyconv_kernel, nk1=nk1, R=R, conv_u=conv_u, chunk=chunk)
    ucb, gcb = u_col0 // LANES, gate_col0 // LANES
    return pl.pallas_call(
        kern,
        grid=(n_ct, B),
        in_specs=[
            pl.BlockSpec((L, LANES), lambda c, b: (b, ucb + c)),
            pl.BlockSpec((L, LANES), lambda c, b: (b, gcb + c)),
            pl.BlockSpec((1, nk1, 2 * P, LANES), lambda c, b: (order, 0, 0, c)),
            _resident((P, mr, R), lambda c, b: (0, 0, 0)),
            _resident((P, R, mr), lambda c, b: (0, 0, 0)),
            _resident((2 * P, 2 * P), lambda c, b: (0, 0)),
            _resident((2 * P, 2 * P), lambda c, b: (0, 0)),
            pl.BlockSpec((3, LANES), lambda c, b: (0, ucb + c if conv_u else c)),
            pl.BlockSpec((1, LANES), lambda c, b: (0, ucb + c if conv_u else c)),
            pl.BlockSpec((3, LANES), lambda c, b: (0, gcb + c)),
            pl.BlockSpec((1, LANES), lambda c, b: (0, gcb + c)),
            pl.BlockSpec((1, LANES), lambda c, b: (0, c)),
        ],
        out_specs=pl.BlockSpec((L, LANES), lambda c, b: (b, c)),
        out_shape=jax.ShapeDtypeStruct((B * L, C), BF16),
        scratch_shapes=[
            pltpu.VMEM((L, LANES), F32),
            pltpu.VMEM((mr * P, LANES), F32),
            pltpu.VMEM((L, LANES), F32),
        ],
        compiler_params=_cparams("parallel", "arbitrary"),
        name=f"hy_conv{order}",
    )(u_arr, gate_arr, kf, tabs["f1"], tabs["g"], tabs["d_fwd"], tabs["d_inv"],
      conv_w, conv_b, conv_w, conv_b, bias[order].reshape(1, C))


def _attn_kernel(q_ref, k_ref, v_ref, lamv_ref, sg_ref, o_ref, vt_scr, m_scr, acc_scr,
                 s0_scr, s1_scr, cm0_scr, cm1_scr, *, lam_init):
    n_kv, vrows, tk = vt_scr.shape
    dv = v_ref.shape[1]
    d = ATT_HEAD_DIM

    @pl.when(pl.program_id(2) == 0)
    def _():
        for c in range(n_kv):
            vt_scr[c, 0:dv, :] = v_ref[c * tk:(c + 1) * tk, :].astype(F32).T.astype(BF16)
            vt_scr[c, dv:vrows, :] = jnp.ones((vrows - dv, tk), BF16)

    qt = q_ref[...].astype(F32).T
    row = lax.broadcasted_iota(jnp.int32, (2 * d, 1), 0)
    qts = (jnp.where(row < d, qt, 0.0).astype(BF16), jnp.where(row >= d, qt, 0.0).astype(BF16))
    m_scr[...] = jnp.full(m_scr.shape, -jnp.inf, F32)
    acc_scr[...] = jnp.zeros(acc_scr.shape, F32)

    def scores(j, s_ref, cm_ref):
        kc = k_ref[pl.ds(pl.multiple_of(j * tk, tk), tk), :]
        for comp in range(2):
            s = jnp.dot(kc, qts[comp], preferred_element_type=F32)
            s_ref[comp] = s
            cm_ref[comp] = jnp.max(s, axis=0, keepdims=True)

    def accumulate(j, s_ref, cm_ref):
        vt = vt_scr[j]
        for comp in range(2):
            m_old = m_scr[comp]
            m_new = jnp.maximum(m_old, cm_ref[comp])
            p = jnp.exp2(s_ref[comp] - m_new).astype(BF16)
            acc_scr[comp] = (jnp.exp2(m_old - m_new) * acc_scr[comp]
                             + jnp.dot(vt, p, preferred_element_type=F32))
            m_scr[comp] = m_new

    even, odd = (s0_scr, cm0_scr), (s1_scr, cm1_scr)
    scores(0, *even)
    if n_kv > 1:
        def body(jj, _):
            j = 2 * jj
            scores(j + 1, *odd)
            accumulate(j, *even)
            scores(j + 2, *even)
            accumulate(j + 1, *odd)
            return 0

        lax.fori_loop(0, n_kv // 2 - 1, body, 0)
        scores(n_kv - 1, *odd)
        accumulate(n_kv - 2, *even)
        accumulate(n_kv - 1, *odd)
    else:
        accumulate(0, *even)

    lv = lamv_ref[...]
    lam = (jnp.exp(jnp.sum(lv[0:1] * lv[1:2], axis=-1, keepdims=True))
           - jnp.exp(jnp.sum(lv[2:3] * lv[3:4], axis=-1, keepdims=True)) + lam_init)
    a1, a2 = acc_scr[0], acc_scr[1]
    o = a1[0:dv] / a1[dv:dv + 1] - lam * (a2[0:dv] / a2[dv:dv + 1])
    ms = jnp.mean(o * o, axis=0, keepdims=True)
    o = o * lax.rsqrt(ms + EPS) * sg_ref[...] * (1.0 - lam_init)
    o_ref[...] = o.T.astype(o_ref.dtype)


def _diff_attention(proj, q_col0, k_col0, v_col0, lam_vecs, subln_g, lam_init, B, L):
    tq = min(512, L)
    tk = min(512, L)
    hw = ATT_V_DIM
    vrows = hw + 16
    qcb, kcb, vcb = q_col0 // hw, k_col0 // hw, v_col0 // hw
    nq = L // tq
    kern = functools.partial(_attn_kernel, lam_init=lam_init)
    return pl.pallas_call(
        kern,
        grid=(B, ATT_HEADS, nq),
        in_specs=[
            pl.BlockSpec((tq, hw), lambda b, h, i: (b * nq + i, qcb + h)),
            pl.BlockSpec((L, hw), lambda b, h, i: (b, kcb + h)),
            pl.BlockSpec((L, hw), lambda b, h, i: (b, vcb + h)),
            pl.BlockSpec((4, ATT_HEAD_DIM), lambda b, h, i: (0, 0)),
            pl.BlockSpec((hw, 1), lambda b, h, i: (0, 0)),
        ],
        out_specs=pl.BlockSpec((tq, hw), lambda b, h, i: (b * nq + i, h)),
        out_shape=jax.ShapeDtypeStruct((B * L, ATT_HEADS * hw), BF16),
        scratch_shapes=[
            pltpu.VMEM((L // tk, vrows, tk), BF16),
            pltpu.VMEM((2, 1, tq), F32),
            pltpu.VMEM((2, vrows, tq), F32),
            pltpu.VMEM((2, tk, tq), F32),
            pltpu.VMEM((2, tk, tq), F32),
            pltpu.VMEM((2, 1, tq), F32),
            pltpu.VMEM((2, 1, tq), F32),
        ],
        compiler_params=_cparams("parallel", "parallel", "arbitrary"),
        name="diff_attn",
    )(proj, proj, proj, lam_vecs, subln_g.reshape(hw, 1))


def _merge_kernel(yh_ref, ya_ref, gh_ref, ga_ref, wph_ref, wpa_ref, wo_ref, x_ref, mod_ref,
                  o_ref, m_scr):
    j = pl.program_id(1)

    @pl.when(j == 0)
    def _():
        mh = jnp.dot(yh_ref[...], wph_ref[...], preferred_element_type=F32)
        ma = jnp.dot(ya_ref[...], wpa_ref[...], preferred_element_type=F32)
        m_scr[...] = (gh_ref[...].astype(F32) * mh + ga_ref[...].astype(F32) * ma).astype(BF16)

    r = jnp.dot(m_scr[...], wo_ref[...], preferred_element_type=F32)
    o_ref[...] = x_ref[...] + mod_ref[0, 2:3, :] * r


def _merge_out(y_hy, y_att, proj, gate_col0, wph, wpa, wo, x2, mod, seq_len):
    T, D = x2.shape
    tm, tn = 512, 512
    bps = seq_len // tm
    gcb = gate_col0 // D
    cw = y_hy.shape[1]
    return pl.pallas_call(
        _merge_kernel,
        grid=(T // tm, D // tn),
        in_specs=[
            pl.BlockSpec((tm, cw), lambda i, j: (i, 0)),
            pl.BlockSpec((tm, cw), lambda i, j: (i, 0)),
            pl.BlockSpec((tm, D), lambda i, j: (i, gcb)),
            pl.BlockSpec((tm, D), lambda i, j: (i, gcb + 1)),
            _resident((cw, D), lambda i, j: (0, 0)),
            _resident((cw, D), lambda i, j: (0, 0)),
            pl.BlockSpec((D, tn), lambda i, j: (0, j)),
            pl.BlockSpec((tm, tn), lambda i, j: (i, j)),
            pl.BlockSpec((1, N_MOD, tn), lambda i, j: (i // bps, 0, j)),
        ],
        out_specs=pl.BlockSpec((tm, tn), lambda i, j: (i, j)),
        out_shape=jax.ShapeDtypeStruct((T, D), F32),
        scratch_shapes=[pltpu.VMEM((tm, D), BF16)],
        compiler_params=_cparams("parallel", "arbitrary"),
        name="merge_out",
    )(y_hy, y_att, proj, proj, wph, wpa, wo, x2, mod)


def _ffn_kernel(x_ref, mod_ref, g_ref, wg_ref, wu_ref, wd_ref, o_ref, h_scr, acc_scr):
    j = pl.program_id(1)

    @pl.when(j == 0)
    def _():
        h = _modulated_norm(x_ref[...], g_ref[...], mod_ref[0, 3:4, :], mod_ref[0, 4:5, :])
        h_scr[...] = h.astype(BF16)
        acc_scr[...] = jnp.zeros_like(acc_scr)

    h = h_scr[...]
    a = jnp.dot(h, wg_ref[...], preferred_element_type=F32)
    u = jnp.dot(h, wu_ref[...], preferred_element_type=F32)
    f = (a * jax.nn.sigmoid(a) * u).astype(BF16)
    acc_scr[...] += jnp.dot(f, wd_ref[...], preferred_element_type=F32)

    @pl.when(j == pl.num_programs(1) - 1)
    def _():
        o_ref[...] = x_ref[...] + mod_ref[0, 5:6, :] * acc_scr[...]


def _ffn(x2, mod, norm_g, wg, wu, wd, seq_len):
    T, D = x2.shape
    H = wg.shape[1]
    tm, th = 512, 512
    bps = seq_len // tm
    return pl.pallas_call(
        _ffn_kernel,
        grid=(T // tm, H // th),
        in_specs=[
            pl.BlockSpec((tm, D), lambda i, j: (i, 0)),
            pl.BlockSpec((1, N_MOD, D), lambda i, j: (i // bps, 0, 0)),
            pl.BlockSpec((1, D), lambda i, j: (0, 0)),
            pl.BlockSpec((D, th), lambda i, j: (0, j)),
            pl.BlockSpec((D, th), lambda i, j: (0, j)),
            pl.BlockSpec((th, D), lambda i, j: (j, 0)),
        ],
        out_specs=pl.BlockSpec((tm, D), lambda i, j: (i, 0)),
        out_shape=jax.ShapeDtypeStruct((T, D), F32),
        scratch_shapes=[pltpu.VMEM((tm, D), BF16), pltpu.VMEM((tm, D), F32)],
        compiler_params=_cparams("parallel", "arbitrary"),
        name="ffn",
    )(x2, mod, norm_g.reshape(1, D), wg, wu, wd)


def kernel(x, c, positions, w_ada, b_ada, norm1_g, w_in, hy_conv_w, hy_conv_b, hy_filt_w1, hy_filt_b1, hy_filt_w2, hy_filt_b2, hy_filt_w3, hy_filt_b3, hy_filt_freq, hy_filt_w_out, hy_bias, q_norm_g, k_norm_g, lam_q1, lam_k1, lam_q2, lam_k2, subln_g, w_proj_hy, w_proj_att, w_out, norm2_g, w_gate, w_up, w_down):
    B, L, D = x.shape
    T = B * L
    depth = w_ada.shape[0]
    s_hy = (HY_ORDER + 1) * HY_WIDTH
    qk_w = ATT_HEADS * 2 * ATT_HEAD_DIM
    q0, k0 = s_hy, s_hy + qk_w
    v0 = k0 + qk_w
    g0 = v0 + ATT_HEADS * ATT_V_DIM
    tabs = _dft_tables(L)
    pos_col = positions.astype(F32).reshape(T, 1)
    x2 = x.reshape(T, D)
    for l in range(depth):
        lam_init = 0.8 - 0.6 * math.exp(-0.3 * l)
        mod = _ada_mod(c, w_ada[l], b_ada[l])
        proj = _in_proj(x2, mod, norm1_g[l], w_in[l].astype(BF16), pos_col,
                        q_norm_g[l], k_norm_g[l], L)

        filt = _hyena_filters(L, hy_filt_w1[l], hy_filt_b1[l], hy_filt_w2[l], hy_filt_b2[l],
                              hy_filt_w3[l], hy_filt_b3[l], hy_filt_freq[l], hy_filt_w_out[l])
        kf = _filter_spectra(filt, tabs)
        cw, cb = hy_conv_w[l], hy_conv_b[l].reshape(1, s_hy)
        z = _hyena_conv(proj, 0, proj, HY_WIDTH, kf, 0, cw, cb, hy_bias[l], tabs, B, L, True)
        y_hy = _hyena_conv(z, 0, proj, 2 * HY_WIDTH, kf, 1, cw, cb, hy_bias[l], tabs, B, L, False)

        lam_vecs = jnp.stack([lam_q1[l], lam_k1[l], lam_q2[l], lam_k2[l]]).astype(F32)
        y_att = _diff_attention(proj, q0, k0, v0, lam_vecs, subln_g[l], lam_init, B, L)

        x2 = _merge_out(y_hy, y_att, proj, g0, w_proj_hy[l].astype(BF16), w_proj_att[l].astype(BF16),
                        w_out[l].astype(BF16), x2, mod, L)
        x2 = _ffn(x2, mod, norm2_g[l], w_gate[l].astype(BF16), w_up[l].astype(BF16),
                  w_down[l].astype(BF16), L)
    return x2.reshape(B, L, D)
```

```python
import functools
import math

import numpy as np
import jax
import jax.numpy as jnp
from jax import lax
from jax.experimental import pallas as pl
from jax.experimental.pallas import tpu as pltpu

F32 = jnp.float32
BF16 = jnp.bfloat16

ATT_HEADS = 8
ATT_HEAD_DIM = 64
ATT_V_DIM = 2 * ATT_HEAD_DIM
ROPE_DIM = ATT_HEAD_DIM // 4
ROPE_THETA = 500000.0
HY_WIDTH = 1024
HY_ORDER = 2
HY_EMB_BANDS = 16
HY_DECAY_TARGET = 1e-2
HY_MIN_DECAY = math.log(HY_DECAY_TARGET) / 0.3
HY_MAX_DECAY = math.log(HY_DECAY_TARGET) / 1.5
N_MOD = 6
EPS = 1e-6
LOG2_E = math.log2(math.e)

LANES = 128
SUBLANES = 8
VMEM_LIMIT_BYTES = 56 * 1024 * 1024

DFT_P = LANES
N2_GROUP = SUBLANES
OUTER_UNROLL = 2


def _inner_unroll(nk1):
    return next((u for u in (13, 8, 5, 4, 3, 2) if nk1 % u == 0), 1)


def _cparams(*sem):
    return pltpu.CompilerParams(dimension_semantics=sem, vmem_limit_bytes=VMEM_LIMIT_BYTES)


def _resident(block_shape, index_map):
    return pl.BlockSpec(block_shape, index_map, pipeline_mode=pl.Buffered(1))


def _ada_kernel(ct_ref, w_ref, b_ref, o_ref):
    w = w_ref[...]
    for b in range(ct_ref.shape[0]):
        cb = ct_ref[b]
        s = cb * jax.nn.sigmoid(cb)
        o_ref[b] = jnp.sum(w * s, axis=0, keepdims=True) + b_ref[...]


def _ada_mod(c, w_ada, b_ada):
    B, D = c.shape
    n = w_ada.shape[1]
    tn = 512
    out = pl.pallas_call(
        _ada_kernel,
        grid=(n // tn,),
        in_specs=[
            pl.BlockSpec((B, D, 1), lambda j: (0, 0, 0)),
            pl.BlockSpec((D, tn), lambda j: (0, j)),
            pl.BlockSpec((1, tn), lambda j: (0, j)),
        ],
        out_specs=pl.BlockSpec((B, 1, tn), lambda j: (0, 0, j)),
        out_shape=jax.ShapeDtypeStruct((B, 1, n), F32),
        compiler_params=_cparams("arbitrary"),
        name="ada_mod",
    )(c.reshape(B, D, 1), w_ada, b_ada.reshape(1, n))
    return out.reshape(B, N_MOD, D)


def _modulated_norm(x, g, shift, scale):
    ms = jnp.mean(x * x, axis=-1, keepdims=True)
    return (x * lax.rsqrt(ms + EPS) * g) * (1.0 + scale) + shift


def _inproj_kernel(x_ref, mod_ref, g_ref, w_ref, pos_ref, qg_ref, kg_ref, inv_ref, bd_ref,
                   o_ref, h_scr, rope_scr, *, jq, jk, jgate):
    j = pl.program_id(1)

    @pl.when(j == 0)
    def _():
        h = _modulated_norm(x_ref[...], g_ref[...], mod_ref[0, 0:1, :], mod_ref[0, 1:2, :])
        h_scr[...] = h.astype(BF16)
        ang = pos_ref[...] * inv_ref[...]
        cos, sin = jnp.cos(ang), jnp.sin(ang)
        lane = lax.broadcasted_iota(jnp.int32, (1, LANES), 1) % ATT_HEAD_DIM
        half = ROPE_DIM // 2
        rope_scr[0] = jnp.where(lane < ROPE_DIM, cos, 1.0)
        rope_scr[1] = jnp.where(lane < half, -sin, jnp.where(lane < ROPE_DIM, sin, 0.0))

    acc = jnp.dot(h_scr[...], w_ref[...], preferred_element_type=F32)

    def qk_norm_rope(gain, out_scale):
        tn = acc.shape[1]
        sq = (acc * acc).astype(BF16)
        ms = jnp.concatenate(
            [jnp.dot(sq[:, g * LANES:(g + 1) * LANES], bd_ref[...], preferred_element_type=F32)
             for g in range(tn // LANES)], axis=1)
        y = acc * lax.rsqrt(ms + EPS) * gain
        reps = tn // LANES
        c_t = jnp.tile(rope_scr[0], (1, reps))
        s_t = jnp.tile(rope_scr[1], (1, reps))
        half = ROPE_DIM // 2
        lane = lax.broadcasted_iota(jnp.int32, (1, tn), 1) % ATT_HEAD_DIM
        y_pair = jnp.where(lane < half, pltpu.roll(y, tn - half, axis=1), pltpu.roll(y, half, axis=1))
        return (y * c_t + y_pair * s_t) * out_scale

    @pl.when(j == jq)
    def _():
        o_ref[...] = qk_norm_rope(qg_ref[...], ATT_HEAD_DIM ** -0.5 * LOG2_E).astype(o_ref.dtype)

    @pl.when(j == jk)
    def _():
        o_ref[...] = qk_norm_rope(kg_ref[...], 1.0).astype(o_ref.dtype)

    @pl.when(j >= jgate)
    def _():
        o_ref[...] = jax.nn.sigmoid(acc).astype(o_ref.dtype)

    @pl.when((j != jq) & (j != jk) & (j < jgate))
    def _():
        o_ref[...] = acc.astype(o_ref.dtype)


def _in_proj(x2, mod, norm_g, w_in_bf, pos_col, q_g, k_g, seq_len):
    T, D = x2.shape
    n = w_in_bf.shape[1]
    tm, tn = 512, 1024
    blocks_per_seq = seq_len // tm
    s_hy = (HY_ORDER + 1) * HY_WIDTH
    qk_w = ATT_HEADS * 2 * ATT_HEAD_DIM
    assert tn == qk_w and s_hy % tn == 0
    jq = s_hy // tn
    jk = jq + 1
    jgate = jk + 2
    half = ROPE_DIM // 2
    inv = ROPE_THETA ** (-jnp.arange(half, dtype=F32) * (2.0 / ROPE_DIM))
    lane = np.arange(LANES) % ATT_HEAD_DIM
    inv_lane = jnp.where(lane < ROPE_DIM, inv[lane % half], 0.0).reshape(1, LANES).astype(F32)
    grp = np.arange(LANES) // ATT_HEAD_DIM
    bd = jnp.asarray((grp[:, None] == grp[None, :]).astype(np.float32) / ATT_HEAD_DIM, BF16)
    tile_g = lambda g: jnp.tile(g.reshape(1, ATT_HEAD_DIM), (1, tn // ATT_HEAD_DIM))
    kern = functools.partial(_inproj_kernel, jq=jq, jk=jk, jgate=jgate)
    return pl.pallas_call(
        kern,
        grid=(T // tm, n // tn),
        in_specs=[
            pl.BlockSpec((tm, D), lambda i, j: (i, 0)),
            pl.BlockSpec((1, N_MOD, D), lambda i, j: (i // blocks_per_seq, 0, 0)),
            pl.BlockSpec((1, D), lambda i, j: (0, 0)),
            pl.BlockSpec((D, tn), lambda i, j: (0, j)),
            pl.BlockSpec((tm, 1), lambda i, j: (i, 0)),
            pl.BlockSpec((1, tn), lambda i, j: (0, 0)),
            pl.BlockSpec((1, tn), lambda i, j: (0, 0)),
            pl.BlockSpec((1, LANES), lambda i, j: (0, 0)),
            pl.BlockSpec((LANES, LANES), lambda i, j: (0, 0)),
        ],
        out_specs=pl.BlockSpec((tm, tn), lambda i, j: (i, j)),
        out_shape=jax.ShapeDtypeStruct((T, n), BF16),
        scratch_shapes=[pltpu.VMEM((tm, D), BF16), pltpu.VMEM((2, tm, LANES), F32)],
        compiler_params=_cparams("parallel", "arbitrary"),
        name="in_proj",
    )(x2, mod, norm_g.reshape(1, D), w_in_bf, pos_col, tile_g(q_g), tile_g(k_g), inv_lane, bd)


def _filter_kernel(z_ref, w1_ref, b1_ref, w2_ref, b2_ref, w3_ref, b3_ref, fr_ref, wo_ref, dl_ref,
                   o_ref, *, n_chunks):
    tl = z_ref.shape[0]
    z = z_ref[...]
    h = jnp.sin(fr_ref[0:1, :] * (jnp.dot(z, w1_ref[...], preferred_element_type=F32) + b1_ref[...]))
    h = jnp.sin(fr_ref[1:2, :] * (jnp.dot(h, w2_ref[...], preferred_element_type=F32) + b2_ref[...]))
    h = jnp.sin(fr_ref[2:3, :] * (jnp.dot(h, w3_ref[...], preferred_element_type=F32) + b3_ref[...]))
    t = z[:, 0:1]
    decay = jnp.exp(-t * dl_ref[...])
    row = pl.program_id(0) * tl + lax.broadcasted_iota(jnp.int32, (tl, 1), 0)
    not_first = (row != 0).astype(F32)
    width = dl_ref.shape[1]
    for ci in range(n_chunks):
        f = jnp.dot(h, wo_ref[:, ci * width:(ci + 1) * width], preferred_element_type=F32) * decay
        if ci % 2 == 1:
            f = f * not_first
        o_ref[ci] = f


def _hyena_filters(L, w1, b1, w2, b2, w3, b3, freq, w_out):
    emb = w1.shape[0]
    hid = w1.shape[1]
    t = jnp.linspace(0.0, 1.0, L, dtype=F32)[:, None]
    w = (2.0 * math.pi / L) * jnp.arange(L, dtype=F32)[:, None]
    bands = jnp.linspace(1e-4, HY_EMB_BANDS - 1, HY_EMB_BANDS, dtype=F32)
    z = jnp.concatenate([t, jnp.cos(bands * w), -jnp.sin(bands * w)], axis=-1)
    z = jnp.pad(z, ((0, 0), (0, LANES - emb)))
    w1p = jnp.pad(w1, ((0, LANES - emb), (0, 0)))
    deltas = jnp.abs(jnp.linspace(HY_MIN_DECAY, HY_MAX_DECAY, HY_WIDTH, dtype=F32)).reshape(1, HY_WIDTH)
    n_chunks = w_out.shape[1] // HY_WIDTH
    tl = min(512, L)
    full = lambda a: pl.BlockSpec(a.shape, lambda i: (0,) * a.ndim)
    b1, b2, b3 = (b.reshape(1, hid) for b in (b1, b2, b3))
    args = (z, w1p, b1, w2, b2, w3, b3, freq, w_out, deltas)
    return pl.pallas_call(
        functools.partial(_filter_kernel, n_chunks=n_chunks),
        grid=(L // tl,),
        in_specs=[pl.BlockSpec((tl, LANES), lambda i: (i, 0))] + [full(a) for a in args[1:]],
        out_specs=pl.BlockSpec((n_chunks, tl, HY_WIDTH), lambda i: (0, i, 0)),
        out_shape=jax.ShapeDtypeStruct((n_chunks, L, HY_WIDTH), F32),
        compiler_params=_cparams("parallel"),
        name="hy_filter",
    )(*args)


def _dft_tables(L):
    P = DFT_P
    R = L // P
    Q = 2 * R
    N = 2 * L
    nk1 = Q // 2 + 1
    mr = -(-2 * nk1 // SUBLANES) * SUBLANES
    n2 = np.arange(P)[:, None, None]
    k1 = np.arange(nk1)[None, :, None]
    n1 = np.arange(R)[None, None, :]
    ph = 2.0 * np.pi * ((k1 * (P * n1 + n2)) % N) / N
    f1 = np.zeros((P, mr, R))
    f1[:, 0:2 * nk1:2, :] = np.cos(ph)
    f1[:, 1:2 * nk1:2, :] = -np.sin(ph)
    ck = np.where((np.arange(nk1) == 0) | (np.arange(nk1) == Q // 2), 1.0, 2.0)[None, :, None] / N
    g = np.zeros((P, R, mr))
    g[:, :, 0:2 * nk1:2] = np.transpose(ck * np.cos(ph), (0, 2, 1))
    g[:, :, 1:2 * nk1:2] = np.transpose(-ck * np.sin(ph), (0, 2, 1))
    a = np.arange(P)
    th = 2.0 * np.pi * ((a[:, None] * a[None, :]) % P) / P
    tr, ti = np.cos(th), -np.sin(th)
    d_fwd = np.block([[tr, -ti], [ti, tr]])
    d_inv = np.block([[tr, ti], [-ti, tr]])
    as_bf = lambda m: jnp.asarray(m, dtype=F32).astype(BF16)
    return dict(f1=as_bf(f1), g=as_bf(g), d_fwd=as_bf(d_fwd), d_inv=as_bf(d_inv), nk1=nk1, mr=mr, R=R)


def _swap_major_sublane(x):
    return jnp.swapaxes(x, 0, 1)


def _fwd_dft(u3, a3, f1_ref, dfwd_ref, nk1, post):
    R, P, _ = u3.shape

    def step1(g, _):
        n0 = pl.multiple_of(g * N2_GROUP, N2_GROUP)
        xt = _swap_major_sublane(u3[:, pl.ds(n0, N2_GROUP), :])
        a = jnp.stack([jnp.dot(f1_ref[n0 + r], xt[r].astype(BF16), preferred_element_type=F32)
                       for r in range(N2_GROUP)])
        a3[:, pl.ds(n0, N2_GROUP), :] = _swap_major_sublane(a)
        return 0

    lax.fori_loop(0, P // N2_GROUP, step1, 0, unroll=OUTER_UNROLL)

    def step3(k1, _):
        a = a3[pl.ds(2 * k1, 2)].reshape(2 * P, LANES).astype(BF16)
        post(k1, jnp.dot(dfwd_ref[...], a, preferred_element_type=F32))
        return 0

    lax.fori_loop(0, nk1, step3, 0, unroll=_inner_unroll(nk1))


def _filtfft_kernel(hf_ref, hb_ref, f1_ref, dfwd_ref, o_ref, a_scr, s_scr, *, nk1):
    P = DFT_P

    def put(k1, x):
        s_scr[k1] = x

    def add_conj(k1, x):
        s = s_scr[k1]
        o_ref[0, k1] = jnp.concatenate([s[0:P] + x[0:P], s[P:2 * P] - x[P:2 * P]], axis=0).astype(o_ref.dtype)

    _fwd_dft(hf_ref.at[0], a_scr, f1_ref, dfwd_ref, nk1, put)
    _fwd_dft(hb_ref.at[0], a_scr, f1_ref, dfwd_ref, nk1, add_conj)


def _filter_spectra(filt, tabs):
    n_f, L, C = filt.shape
    P = DFT_P
    nk1, mr, R = tabs["nk1"], tabs["mr"], tabs["R"]
    filt4 = filt.reshape(n_f, R, P, C)
    kern = functools.partial(_filtfft_kernel, nk1=nk1)
    return pl.pallas_call(
        kern,
        grid=(n_f // 2, C // LANES),
        in_specs=[
            pl.BlockSpec((1, R, P, LANES), lambda o, c: (2 * o, 0, 0, c)),
            pl.BlockSpec((1, R, P, LANES), lambda o, c: (2 * o + 1, 0, 0, c)),
            _resident((P, mr, R), lambda o, c: (0, 0, 0)),
            _resident((2 * P, 2 * P), lambda o, c: (0, 0)),
        ],
        out_specs=pl.BlockSpec((1, nk1, 2 * P, LANES), lambda o, c: (o, 0, 0, c)),
        out_shape=jax.ShapeDtypeStruct((n_f // 2, nk1, 2 * P, C), BF16),
        scratch_shapes=[pltpu.VMEM((mr, P, LANES), F32), pltpu.VMEM((nk1, 2 * P, LANES), F32)],
        compiler_params=_cparams("parallel", "parallel"),
        name="hy_filtfft",
    )(filt4, filt4, tabs["f1"], tabs["d_fwd"])


def _short_conv_chunk(u_ref, r0, rows, L, w_ref, b_ref):
    halo = 16
    c = u_ref[pl.ds(r0, rows), :].astype(F32)
    p0 = pl.multiple_of(jnp.maximum(r0 - halo, 0), halo)
    n0 = pl.multiple_of(jnp.minimum(r0 + rows, L - halo), halo)
    prev = u_ref[pl.ds(p0, halo), :].astype(F32)[halo - 1:halo]
    nxt = u_ref[pl.ds(n0, halo), :].astype(F32)[0:1]
    prev = jnp.where(r0 > 0, prev, 0.0)
    nxt = jnp.where(r0 + rows < L, nxt, 0.0)
    row = lax.broadcasted_iota(jnp.int32, (rows, 1), 0)
    up = jnp.where(row == 0, prev, pltpu.roll(c, 1, axis=0))
    dn = jnp.where(row == rows - 1, nxt, pltpu.roll(c, rows - 1, axis=0))
    return up * w_ref[0:1, :] + c * w_ref[1:2, :] + dn * w_ref[2:3, :] + b_ref[...]


def _hyconv_kernel(u_ref, gt_ref, kf_ref, f1_ref, g_ref, dfwd_ref, dinv_ref, wu_ref, bu_ref,
                   wg_ref, bg_ref, bias_ref, o_ref, u_scr, a_scr, y_scr, *, nk1, conv_u, chunk):
    R, P, _ = u_scr.shape
    L = R * P
    n_chunks = L // chunk
    cr = chunk // P

    def load_u(i, _):
        r0 = pl.multiple_of(i * chunk, chunk)
        if conv_u:
            u = _short_conv_chunk(u_ref, r0, chunk, L, wu_ref, bu_ref)
        else:
            u = u_ref[pl.ds(r0, chunk), :].astype(F32)
        u_scr[pl.ds(i * cr, cr)] = u.reshape(cr, P, LANES)
        return 0

    lax.fori_loop(0, n_chunks, load_u, 0)

    def spectral(k1, x):
        kf = kf_ref[0, k1].astype(F32)
        xr, xi = x[0:P], x[P:2 * P]
        kr, ki = kf[0:P], kf[P:2 * P]
        y = jnp.concatenate([xr * kr - xi * ki, xr * ki + xi * kr], axis=0).astype(BF16)
        a_scr[pl.ds(2 * k1, 2)] = jnp.dot(dinv_ref[...], y, preferred_element_type=F32).reshape(2, P, LANES)

    _fwd_dft(u_scr, a_scr, f1_ref, dfwd_ref, nk1, spectral)

    def inv_outer(g, _):
        n0 = pl.multiple_of(g * N2_GROUP, N2_GROUP)
        bt = _swap_major_sublane(a_scr[:, pl.ds(n0, N2_GROUP), :])
        y = jnp.stack([jnp.dot(g_ref[n0 + r], bt[r].astype(BF16), preferred_element_type=F32)
                       for r in range(N2_GROUP)])
        y_scr[:, pl.ds(n0, N2_GROUP), :] = _swap_major_sublane(y)
        return 0

    lax.fori_loop(0, P // N2_GROUP, inv_outer, 0, unroll=OUTER_UNROLL)

    def epilogue(i, _):
        r0 = pl.multiple_of(i * chunk, chunk)
        gate = _short_conv_chunk(gt_ref, r0, chunk, L, wg_ref, bg_ref)
        rows = pl.ds(i * cr, cr)
        y = y_scr[rows].reshape(chunk, LANES) + u_scr[rows].reshape(chunk, LANES) * bias_ref[...]
        o_ref[pl.ds(r0, chunk), :] = (gate * y).astype(o_ref.dtype)
        return 0

    lax.fori_loop(0, n_chunks, epilogue, 0)


def _hyena_conv(u_arr, u_col0, gate_arr, gate_col0, kf, order, conv_w, conv_b, bias, tabs, B, L, conv_u):
    P = DFT_P
    nk1, mr, R = tabs["nk1"], tabs["mr"], tabs["R"]
    C = HY_WIDTH
    n_ct = C // LANES
    chunk = min(512, L)
    kern = functools.partial(_hyconv_kernel, nk1=nk1, conv_u=conv_u, chunk=chunk)
    ucb, gcb = u_col0 // LANES, gate_col0 // LANES
    return pl.pallas_call(
        kern,
        grid=(n_ct, B),
        in_specs=[
            pl.BlockSpec((L, LANES), lambda c, b: (b, ucb + c)),
            pl.BlockSpec((L, LANES), lambda c, b: (b, gcb + c)),
            pl.BlockSpec((1, nk1, 2 * P, LANES), lambda c, b: (order, 0, 0, c)),
            _resident((P, mr, R), lambda c, b: (0, 0, 0)),
            _resident((P, R, mr), lambda c, b: (0, 0, 0)),
            _resident((2 * P, 2 * P), lambda c, b: (0, 0)),
            _resident((2 * P, 2 * P), lambda c, b: (0, 0)),
            pl.BlockSpec((3, LANES), lambda c, b: (0, ucb + c if conv_u else c)),
            pl.BlockSpec((1, LANES), lambda c, b: (0, ucb + c if conv_u else c)),
            pl.BlockSpec((3, LANES), lambda c, b: (0, gcb + c)),
            pl.BlockSpec((1, LANES), lambda c, b: (0, gcb + c)),
            pl.BlockSpec((1, LANES), lambda c, b: (0, c)),
        ],
        out_specs=pl.BlockSpec((L, LANES), lambda c, b: (b, c)),
        out_shape=jax.ShapeDtypeStruct((B * L, C), BF16),
        scratch_shapes=[
            pltpu.VMEM((R, P, LANES), F32),
            pltpu.VMEM((mr, P, LANES), F32),
            pltpu.VMEM((R, P, LANES), F32),
        ],
        compiler_params=_cparams("parallel", "arbitrary"),
        name=f"hy_conv{order}",
    )(u_arr, gate_arr, kf, tabs["f1"], tabs["g"], tabs["d_fwd"], tabs["d_inv"],
      conv_w, conv_b, conv_w, conv_b, bias[order].reshape(1, C))


def _attn_kernel(q_ref, k_ref, v_ref, lamv_ref, sg_ref, o_ref, vt_scr, m_scr, acc_scr,
                 s0_scr, s1_scr, cm0_scr, cm1_scr, *, lam_init):
    n_kv, vrows, tk = vt_scr.shape
    dv = v_ref.shape[1]
    d = ATT_HEAD_DIM

    @pl.when(pl.program_id(2) == 0)
    def _():
        for c in range(n_kv):
            vt_scr[c, 0:dv, :] = v_ref[c * tk:(c + 1) * tk, :].astype(F32).T.astype(BF16)
            vt_scr[c, dv:vrows, :] = jnp.ones((vrows - dv, tk), BF16)

    qt = q_ref[...].astype(F32).T
    row = lax.broadcasted_iota(jnp.int32, (2 * d, 1), 0)
    qts = (jnp.where(row < d, qt, 0.0).astype(BF16), jnp.where(row >= d, qt, 0.0).astype(BF16))
    m_scr[...] = jnp.full(m_scr.shape, -jnp.inf, F32)
    acc_scr[...] = jnp.zeros(acc_scr.shape, F32)

    def scores(j, s_ref, cm_ref):
        kc = k_ref[pl.ds(pl.multiple_of(j * tk, tk), tk), :]
        for comp in range(2):
            s = jnp.dot(kc, qts[comp], preferred_element_type=F32)
            s_ref[comp] = s
            cm_ref[comp] = jnp.max(s, axis=0, keepdims=True)

    def accumulate(j, s_ref, cm_ref):
        vt = vt_scr[j]
        for comp in range(2):
            m_old = m_scr[comp]
            m_new = jnp.maximum(m_old, cm_ref[comp])
            p = jnp.exp2(s_ref[comp] - m_new).astype(BF16)
            acc_scr[comp] = (jnp.exp2(m_old - m_new) * acc_scr[comp]
                             + jnp.dot(vt, p, preferred_element_type=F32))
            m_scr[comp] = m_new

    even, odd = (s0_scr, cm0_scr), (s1_scr, cm1_scr)
    scores(0, *even)
    if n_kv > 1:
        def body(jj, _):
            j = 2 * jj
            scores(j + 1, *odd)
            accumulate(j, *even)
            scores(j + 2, *even)
            accumulate(j + 1, *odd)
            return 0

        lax.fori_loop(0, n_kv // 2 - 1, body, 0)
        scores(n_kv - 1, *odd)
        accumulate(n_kv - 2, *even)
        accumulate(n_kv - 1, *odd)
    else:
        accumulate(0, *even)

    lv = lamv_ref[...]
    lam = (jnp.exp(jnp.sum(lv[0:1] * lv[1:2], axis=-1, keepdims=True))
           - jnp.exp(jnp.sum(lv[2:3] * lv[3:4], axis=-1, keepdims=True)) + lam_init)
    a1, a2 = acc_scr[0], acc_scr[1]
    o = a1[0:dv] / a1[dv:dv + 1] - lam * (a2[0:dv] / a2[dv:dv + 1])
    ms = jnp.mean(o * o, axis=0, keepdims=True)
    o = o * lax.rsqrt(ms + EPS) * sg_ref[...] * (1.0 - lam_init)
    o_ref[...] = o.T.astype(o_ref.dtype)


def _diff_attention(proj, q_col0, k_col0, v_col0, lam_vecs, subln_g, lam_init, B, L):
    tq = min(1024, L)
    tk = min(512, L)
    hw = ATT_V_DIM
    vrows = hw + 16
    qcb, kcb, vcb = q_col0 // hw, k_col0 // hw, v_col0 // hw
    nq = L // tq
    kern = functools.partial(_attn_kernel, lam_init=lam_init)
    return pl.pallas_call(
        kern,
        grid=(B, ATT_HEADS, nq),
        in_specs=[
            pl.BlockSpec((tq, hw), lambda b, h, i: (b * nq + i, qcb + h)),
            pl.BlockSpec((L, hw), lambda b, h, i: (b, kcb + h)),
            pl.BlockSpec((L, hw), lambda b, h, i: (b, vcb + h)),
            pl.BlockSpec((4, ATT_HEAD_DIM), lambda b, h, i: (0, 0)),
            pl.BlockSpec((hw, 1), lambda b, h, i: (0, 0)),
        ],
        out_specs=pl.BlockSpec((tq, hw), lambda b, h, i: (b * nq + i, h)),
        out_shape=jax.ShapeDtypeStruct((B * L, ATT_HEADS * hw), BF16),
        scratch_shapes=[
            pltpu.VMEM((L // tk, vrows, tk), BF16),
            pltpu.VMEM((2, 1, tq), F32),
            pltpu.VMEM((2, vrows, tq), F32),
            pltpu.VMEM((2, tk, tq), F32),
            pltpu.VMEM((2, tk, tq), F32),
            pltpu.VMEM((2, 1, tq), F32),
            pltpu.VMEM((2, 1, tq), F32),
        ],
        compiler_params=_cparams("parallel", "parallel", "arbitrary"),
        name="diff_attn",
    )(proj, proj, proj, lam_vecs, subln_g.reshape(hw, 1))


def _merge_kernel(yh_ref, ya_ref, gh_ref, ga_ref, wph_ref, wpa_ref, wo_ref, x_ref, mod_ref, o_ref):
    mh = jnp.dot(yh_ref[...], wph_ref[...], preferred_element_type=F32)
    ma = jnp.dot(ya_ref[...], wpa_ref[...], preferred_element_type=F32)
    merged = (gh_ref[...].astype(F32) * mh + ga_ref[...].astype(F32) * ma).astype(BF16)
    r = jnp.dot(merged, wo_ref[...], preferred_element_type=F32)
    o_ref[...] = x_ref[...] + mod_ref[0, 2:3, :] * r


def _merge_out(y_hy, y_att, proj, gate_col0, wph, wpa, wo, x2, mod, seq_len):
    T, D = x2.shape
    tm = 512
    bps = seq_len // tm
    gcb = gate_col0 // D
    cw = y_hy.shape[1]
    return pl.pallas_call(
        _merge_kernel,
        grid=(T // tm,),
        in_specs=[
            pl.BlockSpec((tm, cw), lambda i: (i, 0)),
            pl.BlockSpec((tm, cw), lambda i: (i, 0)),
            pl.BlockSpec((tm, D), lambda i: (i, gcb)),
            pl.BlockSpec((tm, D), lambda i: (i, gcb + 1)),
            _resident((cw, D), lambda i: (0, 0)),
            _resident((cw, D), lambda i: (0, 0)),
            _resident((D, D), lambda i: (0, 0)),
            pl.BlockSpec((tm, D), lambda i: (i, 0)),
            pl.BlockSpec((1, N_MOD, D), lambda i: (i // bps, 0, 0)),
        ],
        out_specs=pl.BlockSpec((tm, D), lambda i: (i, 0)),
        out_shape=jax.ShapeDtypeStruct((T, D), F32),
        compiler_params=_cparams("parallel"),
        name="merge_out",
    )(y_hy, y_att, proj, proj, wph, wpa, wo, x2, mod)


def _ffn_kernel(x_ref, mod_ref, g_ref, wg_ref, wu_ref, wd_ref, o_ref, h_scr, acc_scr):
    j = pl.program_id(1)

    @pl.when(j == 0)
    def _():
        h = _modulated_norm(x_ref[...], g_ref[...], mod_ref[0, 3:4, :], mod_ref[0, 4:5, :])
        h_scr[...] = h.astype(BF16)
        acc_scr[...] = jnp.zeros_like(acc_scr)

    h = h_scr[...]
    a = jnp.dot(h, wg_ref[...], preferred_element_type=F32)
    u = jnp.dot(h, wu_ref[...], preferred_element_type=F32)
    f = (a * jax.nn.sigmoid(a) * u).astype(BF16)
    acc_scr[...] += jnp.dot(f, wd_ref[...], preferred_element_type=F32)

    @pl.when(j == pl.num_programs(1) - 1)
    def _():
        o_ref[...] = x_ref[...] + mod_ref[0, 5:6, :] * acc_scr[...]


def _ffn(x2, mod, norm_g, wg, wu, wd, seq_len):
    T, D = x2.shape
    H = wg.shape[1]
    tm, th = 512, 512
    bps = seq_len // tm
    return pl.pallas_call(
        _ffn_kernel,
        grid=(T // tm, H // th),
        in_specs=[
            pl.BlockSpec((tm, D), lambda i, j: (i, 0)),
            pl.BlockSpec((1, N_MOD, D), lambda i, j: (i // bps, 0, 0)),
            pl.BlockSpec((1, D), lambda i, j: (0, 0)),
            pl.BlockSpec((D, th), lambda i, j: (0, j)),
            pl.BlockSpec((D, th), lambda i, j: (0, j)),
            pl.BlockSpec((th, D), lambda i, j: (j, 0)),
        ],
        out_specs=pl.BlockSpec((tm, D), lambda i, j: (i, 0)),
        out_shape=jax.ShapeDtypeStruct((T, D), F32),
        scratch_shapes=[pltpu.VMEM((tm, D), BF16), pltpu.VMEM((tm, D), F32)],
        compiler_params=_cparams("parallel", "arbitrary"),
        name="ffn",
    )(x2, mod, norm_g.reshape(1, D), wg, wu, wd)


def kernel(x, c, positions, w_ada, b_ada, norm1_g, w_in, hy_conv_w, hy_conv_b, hy_filt_w1, hy_filt_b1, hy_filt_w2, hy_filt_b2, hy_filt_w3, hy_filt_b3, hy_filt_freq, hy_filt_w_out, hy_bias, q_norm_g, k_norm_g, lam_q1, lam_k1, lam_q2, lam_k2, subln_g, w_proj_hy, w_proj_att, w_out, norm2_g, w_gate, w_up, w_down):
    B, L, D = x.shape
    T = B * L
    depth = w_ada.shape[0]
    s_hy = (HY_ORDER + 1) * HY_WIDTH
    qk_w = ATT_HEADS * 2 * ATT_HEAD_DIM
    q0, k0 = s_hy, s_hy + qk_w
    v0 = k0 + qk_w
    g0 = v0 + ATT_HEADS * ATT_V_DIM
    tabs = _dft_tables(L)
    pos_col = positions.astype(F32).reshape(T, 1)
    x2 = x.reshape(T, D)
    for l in range(depth):
        lam_init = 0.8 - 0.6 * math.exp(-0.3 * l)
        mod = _ada_mod(c, w_ada[l], b_ada[l])
        proj = _in_proj(x2, mod, norm1_g[l], w_in[l].astype(BF16), pos_col,
                        q_norm_g[l], k_norm_g[l], L)

        filt = _hyena_filters(L, hy_filt_w1[l], hy_filt_b1[l], hy_filt_w2[l], hy_filt_b2[l],
                              hy_filt_w3[l], hy_filt_b3[l], hy_filt_freq[l], hy_filt_w_out[l])
        kf = _filter_spectra(filt, tabs)
        cw, cb = hy_conv_w[l], hy_conv_b[l].reshape(1, s_hy)
        z = _hyena_conv(proj, 0, proj, HY_WIDTH, kf, 0, cw, cb, hy_bias[l], tabs, B, L, True)
        y_hy = _hyena_conv(z, 0, proj, 2 * HY_WIDTH, kf, 1, cw, cb, hy_bias[l], tabs, B, L, False)

        lam_vecs = jnp.stack([lam_q1[l], lam_k1[l], lam_q2[l], lam_k2[l]]).astype(F32)
        y_att = _diff_attention(proj, q0, k0, v0, lam_vecs, subln_g[l], lam_init, B, L)

        x2 = _merge_out(y_hy, y_att, proj, g0, w_proj_hy[l].astype(BF16), w_proj_att[l].astype(BF16),
                        w_out[l].astype(BF16), x2, mod, L)
        x2 = _ffn(x2, mod, norm2_g[l], w_gate[l].astype(BF16), w_up[l].astype(BF16),
                  w_down[l].astype(BF16), L)
    return x2.reshape(B, L, D)
```

```python
import functools
import math

import numpy as np
import jax
import jax.numpy as jnp
from jax import lax
from jax.experimental import pallas as pl
from jax.experimental.pallas import tpu as pltpu

F32 = jnp.float32
BF16 = jnp.bfloat16

ATT_HEADS = 8
ATT_HEAD_DIM = 64
ATT_V_DIM = 2 * ATT_HEAD_DIM
ROPE_DIM = ATT_HEAD_DIM // 4
ROPE_THETA = 500000.0
HY_WIDTH = 1024
HY_ORDER = 2
HY_EMB_BANDS = 16
HY_DECAY_TARGET = 1e-2
HY_MIN_DECAY = math.log(HY_DECAY_TARGET) / 0.3
HY_MAX_DECAY = math.log(HY_DECAY_TARGET) / 1.5
N_MOD = 6
EPS = 1e-6
LOG2_E = math.log2(math.e)

LANES = 128
SUBLANES = 8
VMEM_LIMIT_BYTES = 56 * 1024 * 1024

ATTN_SCORE_BUFS = 2
ATTN_TRIP_UNROLL = 2
INPROJ_ROW_CHUNK = 256
DFT_P = LANES
N2_GROUP = SUBLANES
OUTER_UNROLL = 2


def _inner_unroll(nk1):
    return next((u for u in (13, 8, 5, 4, 3, 2) if nk1 % u == 0), 1)


def _cparams(*sem):
    return pltpu.CompilerParams(dimension_semantics=sem, vmem_limit_bytes=VMEM_LIMIT_BYTES)


def _resident(block_shape, index_map):
    return pl.BlockSpec(block_shape, index_map, pipeline_mode=pl.Buffered(1))


def _ada_kernel(ct_ref, w_ref, b_ref, o_ref):
    w = w_ref[...]
    for b in range(ct_ref.shape[0]):
        cb = ct_ref[b]
        s = cb * jax.nn.sigmoid(cb)
        o_ref[b] = jnp.sum(w * s, axis=0, keepdims=True) + b_ref[...]


def _ada_mod(c, w_ada, b_ada):
    B, D = c.shape
    n = w_ada.shape[1]
    tn = 512
    out = pl.pallas_call(
        _ada_kernel,
        grid=(n // tn,),
        in_specs=[
            pl.BlockSpec((B, D, 1), lambda j: (0, 0, 0)),
            pl.BlockSpec((D, tn), lambda j: (0, j)),
            pl.BlockSpec((1, tn), lambda j: (0, j)),
        ],
        out_specs=pl.BlockSpec((B, 1, tn), lambda j: (0, 0, j)),
        out_shape=jax.ShapeDtypeStruct((B, 1, n), F32),
        compiler_params=_cparams("arbitrary"),
        name="ada_mod",
    )(c.reshape(B, D, 1), w_ada, b_ada.reshape(1, n))
    return out.reshape(B, N_MOD, D)


def _modulated_norm(x, g, shift, scale):
    ms = jnp.mean(x * x, axis=-1, keepdims=True)
    return (x * lax.rsqrt(ms + EPS) * g) * (1.0 + scale) + shift


def _inproj_kernel(x_ref, mod_ref, g_ref, w_ref, pos_ref, qg_ref, kg_ref, inv_ref, bd_ref,
                   o_ref, h_scr, rope_scr, *, jq, jk, jgate):
    j = pl.program_id(1)

    @pl.when(j == 0)
    def _():
        h = _modulated_norm(x_ref[...], g_ref[...], mod_ref[0, 0:1, :], mod_ref[0, 1:2, :])
        h_scr[...] = h.astype(BF16)
        ang = pos_ref[...] * inv_ref[...]
        cos, sin = jnp.cos(ang), jnp.sin(ang)
        lane = lax.broadcasted_iota(jnp.int32, (1, LANES), 1) % ATT_HEAD_DIM
        half = ROPE_DIM // 2
        rope_scr[0] = jnp.where(lane < ROPE_DIM, cos, 1.0)
        rope_scr[1] = jnp.where(lane < half, -sin, jnp.where(lane < ROPE_DIM, sin, 0.0))

    tm, tn = o_ref.shape

    def project(epilogue):
        for r in range(tm // INPROJ_ROW_CHUNK):
            rows = pl.ds(r * INPROJ_ROW_CHUNK, INPROJ_ROW_CHUNK)
            acc = jnp.dot(h_scr[rows, :], w_ref[...], preferred_element_type=F32)
            o_ref[rows, :] = epilogue(acc, rows).astype(o_ref.dtype)

    def qk_norm_rope(gain_ref, out_scale):
        def epilogue(acc, rows):
            sq = (acc * acc).astype(BF16)
            ms = jnp.concatenate(
                [jnp.dot(sq[:, g * LANES:(g + 1) * LANES], bd_ref[...], preferred_element_type=F32)
                 for g in range(tn // LANES)], axis=1)
            y = acc * lax.rsqrt(ms + EPS) * gain_ref[...]
            reps = tn // LANES
            c_t = jnp.tile(rope_scr[0, rows, :], (1, reps))
            s_t = jnp.tile(rope_scr[1, rows, :], (1, reps))
            half = ROPE_DIM // 2
            lane = lax.broadcasted_iota(jnp.int32, (1, tn), 1) % ATT_HEAD_DIM
            y_pair = jnp.where(lane < half, pltpu.roll(y, tn - half, axis=1), pltpu.roll(y, half, axis=1))
            return (y * c_t + y_pair * s_t) * out_scale
        return epilogue

    @pl.when(j == jq)
    def _():
        project(qk_norm_rope(qg_ref, ATT_HEAD_DIM ** -0.5 * LOG2_E))

    @pl.when(j == jk)
    def _():
        project(qk_norm_rope(kg_ref, 1.0))

    @pl.when(j >= jgate)
    def _():
        project(lambda acc, rows: 0.5 * jnp.tanh(0.5 * acc) + 0.5)

    @pl.when((j != jq) & (j != jk) & (j < jgate))
    def _():
        project(lambda acc, rows: acc)


def _in_proj(x2, mod, norm_g, w_in_bf, pos_col, q_g, k_g, seq_len):
    T, D = x2.shape
    n = w_in_bf.shape[1]
    tm, tn = min(1024, seq_len), 1024
    blocks_per_seq = seq_len // tm
    s_hy = (HY_ORDER + 1) * HY_WIDTH
    qk_w = ATT_HEADS * 2 * ATT_HEAD_DIM
    assert tn == qk_w and s_hy % tn == 0
    jq = s_hy // tn
    jk = jq + 1
    jgate = jk + 2
    half = ROPE_DIM // 2
    inv = ROPE_THETA ** (-jnp.arange(half, dtype=F32) * (2.0 / ROPE_DIM))
    lane = np.arange(LANES) % ATT_HEAD_DIM
    inv_lane = jnp.where(lane < ROPE_DIM, inv[lane % half], 0.0).reshape(1, LANES).astype(F32)
    grp = np.arange(LANES) // ATT_HEAD_DIM
    bd = jnp.asarray((grp[:, None] == grp[None, :]).astype(np.float32) / ATT_HEAD_DIM, BF16)
    tile_g = lambda g: jnp.tile(g.reshape(1, ATT_HEAD_DIM), (1, tn // ATT_HEAD_DIM))
    kern = functools.partial(_inproj_kernel, jq=jq, jk=jk, jgate=jgate)
    return pl.pallas_call(
        kern,
        grid=(T // tm, n // tn),
        in_specs=[
            pl.BlockSpec((tm, D), lambda i, j: (i, 0)),
            pl.BlockSpec((1, N_MOD, D), lambda i, j: (i // blocks_per_seq, 0, 0)),
            pl.BlockSpec((1, D), lambda i, j: (0, 0)),
            pl.BlockSpec((D, tn), lambda i, j: (0, j)),
            pl.BlockSpec((tm, 1), lambda i, j: (i, 0)),
            pl.BlockSpec((1, tn), lambda i, j: (0, 0)),
            pl.BlockSpec((1, tn), lambda i, j: (0, 0)),
            pl.BlockSpec((1, LANES), lambda i, j: (0, 0)),
            pl.BlockSpec((LANES, LANES), lambda i, j: (0, 0)),
        ],
        out_specs=pl.BlockSpec((tm, tn), lambda i, j: (i, j)),
        out_shape=jax.ShapeDtypeStruct((T, n), BF16),
        scratch_shapes=[pltpu.VMEM((tm, D), BF16), pltpu.VMEM((2, tm, LANES), F32)],
        compiler_params=_cparams("parallel", "arbitrary"),
        name="in_proj",
    )(x2, mod, norm_g.reshape(1, D), w_in_bf, pos_col, tile_g(q_g), tile_g(k_g), inv_lane, bd)


def _filter_kernel(z_ref, w1_ref, b1_ref, w2_ref, b2_ref, w3_ref, b3_ref, fr_ref, wo_ref, dl_ref,
                   o_ref, *, n_chunks):
    tl = z_ref.shape[0]
    z = z_ref[...]
    h = jnp.sin(fr_ref[0:1, :] * (jnp.dot(z, w1_ref[...], preferred_element_type=F32) + b1_ref[...]))
    h = jnp.sin(fr_ref[1:2, :] * (jnp.dot(h, w2_ref[...], preferred_element_type=F32) + b2_ref[...]))
    h = jnp.sin(fr_ref[2:3, :] * (jnp.dot(h, w3_ref[...], preferred_element_type=F32) + b3_ref[...]))
    t = z[:, 0:1]
    decay = jnp.exp(-t * dl_ref[...])
    row = pl.program_id(0) * tl + lax.broadcasted_iota(jnp.int32, (tl, 1), 0)
    not_first = (row != 0).astype(F32)
    width = dl_ref.shape[1]
    for ci in range(n_chunks):
        f = jnp.dot(h, wo_ref[:, ci * width:(ci + 1) * width], preferred_element_type=F32) * decay
        if ci % 2 == 1:
            f = f * not_first
        o_ref[ci] = f


def _hyena_filters(L, w1, b1, w2, b2, w3, b3, freq, w_out):
    emb = w1.shape[0]
    hid = w1.shape[1]
    t = jnp.linspace(0.0, 1.0, L, dtype=F32)[:, None]
    w = (2.0 * math.pi / L) * jnp.arange(L, dtype=F32)[:, None]
    bands = jnp.linspace(1e-4, HY_EMB_BANDS - 1, HY_EMB_BANDS, dtype=F32)
    z = jnp.concatenate([t, jnp.cos(bands * w), -jnp.sin(bands * w)], axis=-1)
    z = jnp.pad(z, ((0, 0), (0, LANES - emb)))
    w1p = jnp.pad(w1, ((0, LANES - emb), (0, 0)))
    deltas = jnp.abs(jnp.linspace(HY_MIN_DECAY, HY_MAX_DECAY, HY_WIDTH, dtype=F32)).reshape(1, HY_WIDTH)
    n_chunks = w_out.shape[1] // HY_WIDTH
    tl = min(512, L)
    full = lambda a: pl.BlockSpec(a.shape, lambda i: (0,) * a.ndim)
    b1, b2, b3 = (b.reshape(1, hid) for b in (b1, b2, b3))
    args = (z, w1p, b1, w2, b2, w3, b3, freq, w_out, deltas)
    return pl.pallas_call(
        functools.partial(_filter_kernel, n_chunks=n_chunks),
        grid=(L // tl,),
        in_specs=[pl.BlockSpec((tl, LANES), lambda i: (i, 0))] + [full(a) for a in args[1:]],
        out_specs=pl.BlockSpec((n_chunks, tl, HY_WIDTH), lambda i: (0, i, 0)),
        out_shape=jax.ShapeDtypeStruct((n_chunks, L, HY_WIDTH), F32),
        compiler_params=_cparams("parallel"),
        name="hy_filter",
    )(*args)


def _dft_tables(L):
    P = DFT_P
    R = L // P
    Q = 2 * R
    N = 2 * L
    nk1 = Q // 2 + 1
    mr = -(-2 * nk1 // SUBLANES) * SUBLANES
    n2 = np.arange(P)[:, None, None]
    k1 = np.arange(nk1)[None, :, None]
    n1 = np.arange(R)[None, None, :]
    ph = 2.0 * np.pi * ((k1 * (P * n1 + n2)) % N) / N
    f1 = np.zeros((P, mr, R))
    f1[:, 0:2 * nk1:2, :] = np.cos(ph)
    f1[:, 1:2 * nk1:2, :] = -np.sin(ph)
    ck = np.where((np.arange(nk1) == 0) | (np.arange(nk1) == Q // 2), 1.0, 2.0)[None, :, None] / N
    g = np.zeros((P, R, mr))
    g[:, :, 0:2 * nk1:2] = np.transpose(ck * np.cos(ph), (0, 2, 1))
    g[:, :, 1:2 * nk1:2] = np.transpose(-ck * np.sin(ph), (0, 2, 1))
    a = np.arange(P)
    th = 2.0 * np.pi * ((a[:, None] * a[None, :]) % P) / P
    tr, ti = np.cos(th), -np.sin(th)
    d_fwd = np.block([[tr, -ti], [ti, tr]])
    d_inv = np.block([[tr, ti], [-ti, tr]])
    as_bf = lambda m: jnp.asarray(m, dtype=F32).astype(BF16)
    return dict(f1=as_bf(f1), g=as_bf(g), d_fwd=as_bf(d_fwd), d_inv=as_bf(d_inv), nk1=nk1, mr=mr, R=R)


def _swap_major_sublane(x):
    return jnp.swapaxes(x, 0, 1)


def _fwd_dft(u3s, a3, f1_ref, dfwd_ref, nk1, post):
    R, P, _ = u3s[0].shape
    W = a3.shape[2]

    def step1(g, _):
        n0 = pl.multiple_of(g * N2_GROUP, N2_GROUP)
        x = [u3[:, pl.ds(n0, N2_GROUP), :] for u3 in u3s]
        xt = _swap_major_sublane(x[0] if len(x) == 1 else jnp.concatenate(x, axis=-1))
        a = jnp.stack([jnp.dot(f1_ref[n0 + r], xt[r].astype(BF16), preferred_element_type=F32)
                       for r in range(N2_GROUP)])
        a3[:, pl.ds(n0, N2_GROUP), :] = _swap_major_sublane(a)
        return 0

    lax.fori_loop(0, P // N2_GROUP, step1, 0, unroll=OUTER_UNROLL)

    def step3(k1, _):
        a = a3[pl.ds(2 * k1, 2)].reshape(2 * P, W).astype(BF16)
        post(k1, jnp.dot(dfwd_ref[...], a, preferred_element_type=F32))
        return 0

    lax.fori_loop(0, nk1, step3, 0, unroll=_inner_unroll(nk1))


def _filtfft_kernel(hf_ref, hb_ref, f1_ref, dfwd_ref, o_ref, a_scr, *, nk1):
    P = DFT_P

    def combine(k1, x):
        xf, xb = x[:, 0:LANES], x[:, LANES:2 * LANES]
        o_ref[0, k1] = jnp.concatenate([xf[0:P] + xb[0:P], xf[P:2 * P] - xb[P:2 * P]], axis=0).astype(o_ref.dtype)

    _fwd_dft([hf_ref.at[0], hb_ref.at[0]], a_scr, f1_ref, dfwd_ref, nk1, combine)


def _filter_spectra(filt, tabs):
    n_f, L, C = filt.shape
    P = DFT_P
    nk1, mr, R = tabs["nk1"], tabs["mr"], tabs["R"]
    filt4 = filt.reshape(n_f, R, P, C)
    kern = functools.partial(_filtfft_kernel, nk1=nk1)
    return pl.pallas_call(
        kern,
        grid=(n_f // 2, C // LANES),
        in_specs=[
            pl.BlockSpec((1, R, P, LANES), lambda o, c: (2 * o, 0, 0, c)),
            pl.BlockSpec((1, R, P, LANES), lambda o, c: (2 * o + 1, 0, 0, c)),
            _resident((P, mr, R), lambda o, c: (0, 0, 0)),
            _resident((2 * P, 2 * P), lambda o, c: (0, 0)),
        ],
        out_specs=pl.BlockSpec((1, nk1, 2 * P, LANES), lambda o, c: (o, 0, 0, c)),
        out_shape=jax.ShapeDtypeStruct((n_f // 2, nk1, 2 * P, C), BF16),
        scratch_shapes=[pltpu.VMEM((mr, P, 2 * LANES), F32)],
        compiler_params=_cparams("parallel", "parallel"),
        name="hy_filtfft",
    )(filt4, filt4, tabs["f1"], tabs["d_fwd"])


def _short_conv_chunk(u_ref, r0, rows, L, w_ref, b_ref):
    halo = 16
    c = u_ref[pl.ds(r0, rows), :].astype(F32)
    p0 = pl.multiple_of(jnp.maximum(r0 - halo, 0), halo)
    n0 = pl.multiple_of(jnp.minimum(r0 + rows, L - halo), halo)
    prev = u_ref[pl.ds(p0, halo), :].astype(F32)[halo - 1:halo]
    nxt = u_ref[pl.ds(n0, halo), :].astype(F32)[0:1]
    prev = jnp.where(r0 > 0, prev, 0.0)
    nxt = jnp.where(r0 + rows < L, nxt, 0.0)
    row = lax.broadcasted_iota(jnp.int32, (rows, 1), 0)
    up = jnp.where(row == 0, prev, pltpu.roll(c, 1, axis=0))
    dn = jnp.where(row == rows - 1, nxt, pltpu.roll(c, rows - 1, axis=0))
    return up * w_ref[0:1, :] + c * w_ref[1:2, :] + dn * w_ref[2:3, :] + b_ref[...]


def _hyconv_kernel(u_ref, gt_ref, kf_ref, f1_ref, g_ref, dfwd_ref, dinv_ref, wu_ref, bu_ref,
                   wg_ref, bg_ref, bias_ref, o_ref, u_scr, a_scr, y_scr, *, nk1, conv_u, chunk):
    R, P, _ = u_scr.shape
    L = R * P
    n_chunks = L // chunk
    cr = chunk // P

    def load_u(i, _):
        r0 = pl.multiple_of(i * chunk, chunk)
        if conv_u:
            u = _short_conv_chunk(u_ref, r0, chunk, L, wu_ref, bu_ref)
        else:
            u = u_ref[pl.ds(r0, chunk), :].astype(F32)
        u_scr[pl.ds(i * cr, cr)] = u.reshape(cr, P, LANES)
        return 0

    lax.fori_loop(0, n_chunks, load_u, 0)

    def spectral(k1, x):
        kf = kf_ref[0, k1].astype(F32)
        xr, xi = x[0:P], x[P:2 * P]
        kr, ki = kf[0:P], kf[P:2 * P]
        y = jnp.concatenate([xr * kr - xi * ki, xr * ki + xi * kr], axis=0).astype(BF16)
        a_scr[pl.ds(2 * k1, 2)] = jnp.dot(dinv_ref[...], y, preferred_element_type=F32).reshape(2, P, LANES)

    _fwd_dft([u_scr], a_scr, f1_ref, dfwd_ref, nk1, spectral)

    def inv_outer(g, _):
        n0 = pl.multiple_of(g * N2_GROUP, N2_GROUP)
        bt = _swap_major_sublane(a_scr[:, pl.ds(n0, N2_GROUP), :])
        y = jnp.stack([jnp.dot(g_ref[n0 + r], bt[r].astype(BF16), preferred_element_type=F32)
                       for r in range(N2_GROUP)])
        y_scr[:, pl.ds(n0, N2_GROUP), :] = _swap_major_sublane(y)
        return 0

    lax.fori_loop(0, P // N2_GROUP, inv_outer, 0, unroll=OUTER_UNROLL)

    def epilogue(i, _):
        r0 = pl.multiple_of(i * chunk, chunk)
        gate = _short_conv_chunk(gt_ref, r0, chunk, L, wg_ref, bg_ref)
        rows = pl.ds(i * cr, cr)
        y = y_scr[rows].reshape(chunk, LANES) + u_scr[rows].reshape(chunk, LANES) * bias_ref[...]
        o_ref[pl.ds(r0, chunk), :] = (gate * y).astype(o_ref.dtype)
        return 0

    lax.fori_loop(0, n_chunks, epilogue, 0)


def _hyena_conv(u_arr, u_col0, gate_arr, gate_col0, kf, order, conv_w, conv_b, bias, tabs, B, L, conv_u):
    P = DFT_P
    nk1, mr, R = tabs["nk1"], tabs["mr"], tabs["R"]
    C = HY_WIDTH
    n_ct = C // LANES
    chunk = min(512, L)
    kern = functools.partial(_hyconv_kernel, nk1=nk1, conv_u=conv_u, chunk=chunk)
    ucb, gcb = u_col0 // LANES, gate_col0 // LANES
    return pl.pallas_call(
        kern,
        grid=(n_ct, B),
        in_specs=[
            pl.BlockSpec((L, LANES), lambda c, b: (b, ucb + c)),
            pl.BlockSpec((L, LANES), lambda c, b: (b, gcb + c)),
            pl.BlockSpec((1, nk1, 2 * P, LANES), lambda c, b: (order, 0, 0, c)),
            _resident((P, mr, R), lambda c, b: (0, 0, 0)),
            _resident((P, R, mr), lambda c, b: (0, 0, 0)),
            _resident((2 * P, 2 * P), lambda c, b: (0, 0)),
            _resident((2 * P, 2 * P), lambda c, b: (0, 0)),
            pl.BlockSpec((3, LANES), lambda c, b: (0, ucb + c if conv_u else c)),
            pl.BlockSpec((1, LANES), lambda c, b: (0, ucb + c if conv_u else c)),
            pl.BlockSpec((3, LANES), lambda c, b: (0, gcb + c)),
            pl.BlockSpec((1, LANES), lambda c, b: (0, gcb + c)),
            pl.BlockSpec((1, LANES), lambda c, b: (0, c)),
        ],
        out_specs=pl.BlockSpec((L, LANES), lambda c, b: (b, c)),
        out_shape=jax.ShapeDtypeStruct((B * L, C), BF16),
        scratch_shapes=[
            pltpu.VMEM((R, P, LANES), F32),
            pltpu.VMEM((mr, P, LANES), F32),
            pltpu.VMEM((R, P, LANES), F32),
        ],
        compiler_params=_cparams("parallel", "arbitrary"),
        name=f"hy_conv{order}",
    )(u_arr, gate_arr, kf, tabs["f1"], tabs["g"], tabs["d_fwd"], tabs["d_inv"],
      conv_w, conv_b, conv_w, conv_b, bias[order].reshape(1, C))


def _attn_kernel(q_ref, k_ref, v_ref, lamv_ref, sg_ref, o_ref, vt_scr, m_scr, acc_scr,
                 *score_bufs, lam_init):
    n_kv, vrows, tk = vt_scr.shape
    dv = v_ref.shape[1]
    d = ATT_HEAD_DIM

    @pl.when(pl.program_id(2) == 0)
    def _():
        for c in range(n_kv):
            vt_scr[c, 0:dv, :] = v_ref[c * tk:(c + 1) * tk, :].astype(F32).T.astype(BF16)
            vt_scr[c, dv:vrows, :] = jnp.ones((vrows - dv, tk), BF16)

    qt = q_ref[...].astype(F32).T
    row = lax.broadcasted_iota(jnp.int32, (2 * d, 1), 0)
    qts = (jnp.where(row < d, qt, 0.0).astype(BF16), jnp.where(row >= d, qt, 0.0).astype(BF16))
    m_scr[...] = jnp.full(m_scr.shape, -jnp.inf, F32)
    acc_scr[...] = jnp.zeros(acc_scr.shape, F32)

    def scores(j, s_ref, cm_ref):
        kc = k_ref[pl.ds(pl.multiple_of(j * tk, tk), tk), :]
        for comp in range(2):
            s = jnp.dot(kc, qts[comp], preferred_element_type=F32)
            s_ref[comp] = s
            cm_ref[comp] = jnp.max(s, axis=0, keepdims=True)

    def accumulate(j, s_ref, cm_ref):
        vt = vt_scr[j]
        for comp in range(2):
            m_old = m_scr[comp]
            m_new = jnp.maximum(m_old, cm_ref[comp])
            p = jnp.exp2(s_ref[comp] - m_new).astype(BF16)
            acc_scr[comp] = (jnp.exp2(m_old - m_new) * acc_scr[comp]
                             + jnp.dot(vt, p, preferred_element_type=F32))
            m_scr[comp] = m_new

    nbuf = len(score_bufs) // 2
    bufs = [(score_bufs[i], score_bufs[nbuf + i]) for i in range(nbuf)]
    depth = nbuf - 1
    for c in range(min(depth, n_kv)):
        scores(c, *bufs[c % nbuf])
    trips = max((n_kv - depth) // nbuf, 0)

    def body(jj, _):
        j = jj * nbuf
        for t in range(nbuf):
            scores(j + t + depth, *bufs[(t + depth) % nbuf])
            accumulate(j + t, *bufs[t])
        return 0

    if trips > 0:
        lax.fori_loop(0, trips, body, 0, unroll=ATTN_TRIP_UNROLL)
    for c in range(trips * nbuf, n_kv):
        if c + depth < n_kv:
            scores(c + depth, *bufs[(c + depth) % nbuf])
        accumulate(c, *bufs[c % nbuf])

    lv = lamv_ref[...]
    lam = (jnp.exp(jnp.sum(lv[0:1] * lv[1:2], axis=-1, keepdims=True))
           - jnp.exp(jnp.sum(lv[2:3] * lv[3:4], axis=-1, keepdims=True)) + lam_init)
    a1, a2 = acc_scr[0], acc_scr[1]
    o = a1[0:dv] / a1[dv:dv + 1] - lam * (a2[0:dv] / a2[dv:dv + 1])
    ms = jnp.mean(o * o, axis=0, keepdims=True)
    o = o * lax.rsqrt(ms + EPS) * sg_ref[...] * (1.0 - lam_init)
    o_ref[...] = o.T.astype(o_ref.dtype)


def _diff_attention(proj, q_col0, k_col0, v_col0, lam_vecs, subln_g, lam_init, B, L):
    tq = min(1024, L)
    tk = min(512, L)
    hw = ATT_V_DIM
    vrows = hw + 16
    qcb, kcb, vcb = q_col0 // hw, k_col0 // hw, v_col0 // hw
    nq = L // tq
    kern = functools.partial(_attn_kernel, lam_init=lam_init)
    return pl.pallas_call(
        kern,
        grid=(B, ATT_HEADS, nq),
        in_specs=[
            pl.BlockSpec((tq, hw), lambda b, h, i: (b * nq + i, qcb + h)),
            pl.BlockSpec((L, hw), lambda b, h, i: (b, kcb + h)),
            pl.BlockSpec((L, hw), lambda b, h, i: (b, vcb + h)),
            pl.BlockSpec((4, ATT_HEAD_DIM), lambda b, h, i: (0, 0)),
            pl.BlockSpec((hw, 1), lambda b, h, i: (0, 0)),
        ],
        out_specs=pl.BlockSpec((tq, hw), lambda b, h, i: (b * nq + i, h)),
        out_shape=jax.ShapeDtypeStruct((B * L, ATT_HEADS * hw), BF16),
        scratch_shapes=[
            pltpu.VMEM((L // tk, vrows, tk), BF16),
            pltpu.VMEM((2, 1, tq), F32),
            pltpu.VMEM((2, vrows, tq), F32),
        ] + [pltpu.VMEM((2, tk, tq), F32)] * ATTN_SCORE_BUFS + [pltpu.VMEM((2, 1, tq), F32)] * ATTN_SCORE_BUFS,
        compiler_params=_cparams("parallel", "parallel", "arbitrary"),
        name="diff_attn",
    )(proj, proj, proj, lam_vecs, subln_g.reshape(hw, 1))


def _merge_kernel(yh_ref, ya_ref, gh_ref, ga_ref, wph_ref, wpa_ref, wo_ref, x_ref, mod_ref, o_ref):
    mh = jnp.dot(yh_ref[...], wph_ref[...], preferred_element_type=F32)
    ma = jnp.dot(ya_ref[...], wpa_ref[...], preferred_element_type=F32)
    merged = (gh_ref[...].astype(F32) * mh + ga_ref[...].astype(F32) * ma).astype(BF16)
    r = jnp.dot(merged, wo_ref[...], preferred_element_type=F32)
    o_ref[...] = x_ref[...] + mod_ref[0, 2:3, :] * r


def _merge_out(y_hy, y_att, proj, gate_col0, wph, wpa, wo, x2, mod, seq_len):
    T, D = x2.shape
    tm = 512
    bps = seq_len // tm
    gcb = gate_col0 // D
    cw = y_hy.shape[1]
    return pl.pallas_call(
        _merge_kernel,
        grid=(T // tm,),
        in_specs=[
            pl.BlockSpec((tm, cw), lambda i: (i, 0)),
            pl.BlockSpec((tm, cw), lambda i: (i, 0)),
            pl.BlockSpec((tm, D), lambda i: (i, gcb)),
            pl.BlockSpec((tm, D), lambda i: (i, gcb + 1)),
            _resident((cw, D), lambda i: (0, 0)),
            _resident((cw, D), lambda i: (0, 0)),
            _resident((D, D), lambda i: (0, 0)),
            pl.BlockSpec((tm, D), lambda i: (i, 0)),
            pl.BlockSpec((1, N_MOD, D), lambda i: (i // bps, 0, 0)),
        ],
        out_specs=pl.BlockSpec((tm, D), lambda i: (i, 0)),
        out_shape=jax.ShapeDtypeStruct((T, D), F32),
        compiler_params=_cparams("parallel"),
        name="merge_out",
    )(y_hy, y_att, proj, proj, wph, wpa, wo, x2, mod)


def _ffn_kernel(x_ref, mod_ref, g_ref, wg_ref, wu_ref, wd_ref, o_ref, h_scr, acc_scr):
    j = pl.program_id(1)

    @pl.when(j == 0)
    def _():
        h = _modulated_norm(x_ref[...], g_ref[...], mod_ref[0, 3:4, :], mod_ref[0, 4:5, :])
        h_scr[...] = h.astype(BF16)
        acc_scr[...] = jnp.zeros_like(acc_scr)

    h = h_scr[...]
    a = jnp.dot(h, wg_ref[...], preferred_element_type=F32)
    u = jnp.dot(h, wu_ref[...], preferred_element_type=F32)
    f = (a * jax.nn.sigmoid(a) * u).astype(BF16)
    acc_scr[...] += jnp.dot(f, wd_ref[...], preferred_element_type=F32)

    @pl.when(j == pl.num_programs(1) - 1)
    def _():
        o_ref[...] = x_ref[...] + mod_ref[0, 5:6, :] * acc_scr[...]


def _ffn(x2, mod, norm_g, wg, wu, wd, seq_len):
    T, D = x2.shape
    H = wg.shape[1]
    tm, th = 512, 512
    bps = seq_len // tm
    return pl.pallas_call(
        _ffn_kernel,
        grid=(T // tm, H // th),
        in_specs=[
            pl.BlockSpec((tm, D), lambda i, j: (i, 0)),
            pl.BlockSpec((1, N_MOD, D), lambda i, j: (i // bps, 0, 0)),
            pl.BlockSpec((1, D), lambda i, j: (0, 0)),
            pl.BlockSpec((D, th), lambda i, j: (0, j)),
            pl.BlockSpec((D, th), lambda i, j: (0, j)),
            pl.BlockSpec((th, D), lambda i, j: (j, 0)),
        ],
        out_specs=pl.BlockSpec((tm, D), lambda i, j: (i, 0)),
        out_shape=jax.ShapeDtypeStruct((T, D), F32),
        scratch_shapes=[pltpu.VMEM((tm, D), BF16), pltpu.VMEM((tm, D), F32)],
        compiler_params=_cparams("parallel", "arbitrary"),
        name="ffn",
    )(x2, mod, norm_g.reshape(1, D), wg, wu, wd)


def kernel(x, c, positions, w_ada, b_ada, norm1_g, w_in, hy_conv_w, hy_conv_b, hy_filt_w1, hy_filt_b1, hy_filt_w2, hy_filt_b2, hy_filt_w3, hy_filt_b3, hy_filt_freq, hy_filt_w_out, hy_bias, q_norm_g, k_norm_g, lam_q1, lam_k1, lam_q2, lam_k2, subln_g, w_proj_hy, w_proj_att, w_out, norm2_g, w_gate, w_up, w_down):
    B, L, D = x.shape
    T = B * L
    depth = w_ada.shape[0]
    s_hy = (HY_ORDER + 1) * HY_WIDTH
    qk_w = ATT_HEADS * 2 * ATT_HEAD_DIM
    q0, k0 = s_hy, s_hy + qk_w
    v0 = k0 + qk_w
    g0 = v0 + ATT_HEADS * ATT_V_DIM
    tabs = _dft_tables(L)
    pos_col = positions.astype(F32).reshape(T, 1)
    x2 = x.reshape(T, D)
    for l in range(depth):
        lam_init = 0.8 - 0.6 * math.exp(-0.3 * l)
        mod = _ada_mod(c, w_ada[l], b_ada[l])
        proj = _in_proj(x2, mod, norm1_g[l], w_in[l].astype(BF16), pos_col,
                        q_norm_g[l], k_norm_g[l], L)

        filt = _hyena_filters(L, hy_filt_w1[l], hy_filt_b1[l], hy_filt_w2[l], hy_filt_b2[l],
                              hy_filt_w3[l], hy_filt_b3[l], hy_filt_freq[l], hy_filt_w_out[l])
        kf = _filter_spectra(filt, tabs)
        cw, cb = hy_conv_w[l], hy_conv_b[l].reshape(1, s_hy)
        z = _hyena_conv(proj, 0, proj, HY_WIDTH, kf, 0, cw, cb, hy_bias[l], tabs, B, L, True)
        y_hy = _hyena_conv(z, 0, proj, 2 * HY_WIDTH, kf, 1, cw, cb, hy_bias[l], tabs, B, L, False)

        lam_vecs = jnp.stack([lam_q1[l], lam_k1[l], lam_q2[l], lam_k2[l]]).astype(F32)
        y_att = _diff_attention(proj, q0, k0, v0, lam_vecs, subln_g[l], lam_init, B, L)

        x2 = _merge_out(y_hy, y_att, proj, g0, w_proj_hy[l].astype(BF16), w_proj_att[l].astype(BF16),
                        w_out[l].astype(BF16), x2, mod, L)
        x2 = _ffn(x2, mod, norm2_g[l], w_gate[l].astype(BF16), w_up[l].astype(BF16),
                  w_down[l].astype(BF16), L)
    return x2.reshape(B, L, D)
```

```python
import functools
import math

import numpy as np
import jax
import jax.numpy as jnp
from jax import lax
from jax.experimental import pallas as pl
from jax.experimental.pallas import tpu as pltpu

F32 = jnp.float32
BF16 = jnp.bfloat16

ATT_HEADS = 8
ATT_HEAD_DIM = 64
ATT_V_DIM = 2 * ATT_HEAD_DIM
ROPE_DIM = ATT_HEAD_DIM // 4
ROPE_THETA = 500000.0
HY_WIDTH = 1024
HY_ORDER = 2
HY_EMB_BANDS = 16
HY_DECAY_TARGET = 1e-2
HY_MIN_DECAY = math.log(HY_DECAY_TARGET) / 0.3
HY_MAX_DECAY = math.log(HY_DECAY_TARGET) / 1.5
N_MOD = 6
EPS = 1e-6
LOG2_E = math.log2(math.e)

LANES = 128
SUBLANES = 8
MXU_WIDTH = 256
VMEM_LIMIT_BYTES = 56 * 1024 * 1024

ATTN_SCORE_BUFS = 2
ATTN_QUERY_TILE = MXU_WIDTH
ATTN_TRIP_UNROLL = 2
FFN_HIDDEN_TILE = MXU_WIDTH
INPROJ_ROW_CHUNK = 256
DFT_P = LANES
N2_GROUP = SUBLANES
OUTER_UNROLL = 4


def _inner_unroll(nk1):
    return next((u for u in (13, 8, 5, 4, 3, 2) if nk1 % u == 0), 1)


def _cparams(*sem):
    return pltpu.CompilerParams(dimension_semantics=sem, vmem_limit_bytes=VMEM_LIMIT_BYTES)


def _resident(block_shape, index_map):
    return pl.BlockSpec(block_shape, index_map, pipeline_mode=pl.Buffered(1))


def _ada_kernel(ct_ref, w_ref, b_ref, o_ref):
    w = w_ref[...]
    for b in range(ct_ref.shape[0]):
        cb = ct_ref[b]
        s = cb * jax.nn.sigmoid(cb)
        o_ref[b] = jnp.sum(w * s, axis=0, keepdims=True) + b_ref[...]


def _ada_mod(c, w_ada, b_ada):
    B, D = c.shape
    n = w_ada.shape[1]
    tn = 512
    out = pl.pallas_call(
        _ada_kernel,
        grid=(n // tn,),
        in_specs=[
            pl.BlockSpec((B, D, 1), lambda j: (0, 0, 0)),
            pl.BlockSpec((D, tn), lambda j: (0, j)),
            pl.BlockSpec((1, tn), lambda j: (0, j)),
        ],
        out_specs=pl.BlockSpec((B, 1, tn), lambda j: (0, 0, j)),
        out_shape=jax.ShapeDtypeStruct((B, 1, n), F32),
        compiler_params=_cparams("arbitrary"),
        name="ada_mod",
    )(c.reshape(B, D, 1), w_ada, b_ada.reshape(1, n))
    return out.reshape(B, N_MOD, D)


def _modulated_norm(x, g, shift, scale):
    ms = jnp.mean(x * x, axis=-1, keepdims=True)
    return (x * lax.rsqrt(ms + EPS) * g) * (1.0 + scale) + shift


def _inproj_kernel(x_ref, mod_ref, g_ref, w_ref, pos_ref, qg_ref, kg_ref, inv_ref, bd_ref,
                   o_ref, h_scr, rope_scr, *, jq, jk, jgate):
    j = pl.program_id(1)

    @pl.when(j == 0)
    def _():
        h = _modulated_norm(x_ref[...], g_ref[...], mod_ref[0, 0:1, :], mod_ref[0, 1:2, :])
        h_scr[...] = h.astype(BF16)
        ang = pos_ref[...] * inv_ref[...]
        cos, sin = jnp.cos(ang), jnp.sin(ang)
        lane = lax.broadcasted_iota(jnp.int32, (1, LANES), 1) % ATT_HEAD_DIM
        half = ROPE_DIM // 2
        rope_scr[0] = jnp.where(lane < ROPE_DIM, cos, 1.0)
        rope_scr[1] = jnp.where(lane < half, -sin, jnp.where(lane < ROPE_DIM, sin, 0.0))

    tm, tn = o_ref.shape

    def project(epilogue):
        for r in range(tm // INPROJ_ROW_CHUNK):
            rows = pl.ds(r * INPROJ_ROW_CHUNK, INPROJ_ROW_CHUNK)
            acc = jnp.dot(h_scr[rows, :], w_ref[...], preferred_element_type=F32)
            o_ref[rows, :] = epilogue(acc, rows).astype(o_ref.dtype)

    def qk_norm_rope(gain_ref, out_scale):
        def epilogue(acc, rows):
            sq = (acc * acc).astype(BF16)
            bw = bd_ref.shape[0]
            ms = jnp.concatenate(
                [jnp.dot(sq[:, g * bw:(g + 1) * bw], bd_ref[...], preferred_element_type=F32)
                 for g in range(tn // bw)], axis=1)
            y = acc * lax.rsqrt(ms + EPS) * gain_ref[...]
            reps = tn // LANES
            c_t = jnp.tile(rope_scr[0, rows, :], (1, reps))
            s_t = jnp.tile(rope_scr[1, rows, :], (1, reps))
            half = ROPE_DIM // 2
            lane = lax.broadcasted_iota(jnp.int32, (1, tn), 1) % ATT_HEAD_DIM
            y_pair = jnp.where(lane < half, pltpu.roll(y, tn - half, axis=1), pltpu.roll(y, half, axis=1))
            return (y * c_t + y_pair * s_t) * out_scale
        return epilogue

    @pl.when(j == jq)
    def _():
        project(qk_norm_rope(qg_ref, ATT_HEAD_DIM ** -0.5 * LOG2_E))

    @pl.when(j == jk)
    def _():
        project(qk_norm_rope(kg_ref, 1.0))

    @pl.when(j >= jgate)
    def _():
        project(lambda acc, rows: 0.5 * jnp.tanh(0.5 * acc) + 0.5)

    @pl.when((j != jq) & (j != jk) & (j < jgate))
    def _():
        project(lambda acc, rows: acc)


def _in_proj(x2, mod, norm_g, w_in_bf, pos_col, q_g, k_g, seq_len):
    T, D = x2.shape
    n = w_in_bf.shape[1]
    tm, tn = min(1024, seq_len), 1024
    blocks_per_seq = seq_len // tm
    s_hy = (HY_ORDER + 1) * HY_WIDTH
    qk_w = ATT_HEADS * 2 * ATT_HEAD_DIM
    assert tn == qk_w and s_hy % tn == 0
    jq = s_hy // tn
    jk = jq + 1
    jgate = jk + 2
    half = ROPE_DIM // 2
    inv = ROPE_THETA ** (-jnp.arange(half, dtype=F32) * (2.0 / ROPE_DIM))
    lane = np.arange(LANES) % ATT_HEAD_DIM
    inv_lane = jnp.where(lane < ROPE_DIM, inv[lane % half], 0.0).reshape(1, LANES).astype(F32)
    grp = np.arange(MXU_WIDTH) // ATT_HEAD_DIM
    bd = jnp.asarray((grp[:, None] == grp[None, :]).astype(np.float32) / ATT_HEAD_DIM, BF16)
    tile_g = lambda g: jnp.tile(g.reshape(1, ATT_HEAD_DIM), (1, tn // ATT_HEAD_DIM))
    kern = functools.partial(_inproj_kernel, jq=jq, jk=jk, jgate=jgate)
    return pl.pallas_call(
        kern,
        grid=(T // tm, n // tn),
        in_specs=[
            pl.BlockSpec((tm, D), lambda i, j: (i, 0)),
            pl.BlockSpec((1, N_MOD, D), lambda i, j: (i // blocks_per_seq, 0, 0)),
            pl.BlockSpec((1, D), lambda i, j: (0, 0)),
            pl.BlockSpec((D, tn), lambda i, j: (0, j)),
            pl.BlockSpec((tm, 1), lambda i, j: (i, 0)),
            pl.BlockSpec((1, tn), lambda i, j: (0, 0)),
            pl.BlockSpec((1, tn), lambda i, j: (0, 0)),
            pl.BlockSpec((1, LANES), lambda i, j: (0, 0)),
            pl.BlockSpec((MXU_WIDTH, MXU_WIDTH), lambda i, j: (0, 0)),
        ],
        out_specs=pl.BlockSpec((tm, tn), lambda i, j: (i, j)),
        out_shape=jax.ShapeDtypeStruct((T, n), BF16),
        scratch_shapes=[pltpu.VMEM((tm, D), BF16), pltpu.VMEM((2, tm, LANES), F32)],
        compiler_params=_cparams("parallel", "arbitrary"),
        name="in_proj",
    )(x2, mod, norm_g.reshape(1, D), w_in_bf, pos_col, tile_g(q_g), tile_g(k_g), inv_lane, bd)


def _filter_kernel(z_ref, w1_ref, b1_ref, w2_ref, b2_ref, w3_ref, b3_ref, fr_ref, wo_ref, dl_ref,
                   o_ref, *, n_chunks):
    tl = z_ref.shape[0]
    z = z_ref[...]
    h = jnp.sin(fr_ref[0:1, :] * (jnp.dot(z, w1_ref[...], preferred_element_type=F32) + b1_ref[...]))
    h = jnp.sin(fr_ref[1:2, :] * (jnp.dot(h, w2_ref[...], preferred_element_type=F32) + b2_ref[...]))
    h = jnp.sin(fr_ref[2:3, :] * (jnp.dot(h, w3_ref[...], preferred_element_type=F32) + b3_ref[...]))
    t = z[:, 0:1]
    decay = jnp.exp(-t * dl_ref[...])
    row = pl.program_id(0) * tl + lax.broadcasted_iota(jnp.int32, (tl, 1), 0)
    not_first = (row != 0).astype(F32)
    width = dl_ref.shape[1]
    for ci in range(n_chunks):
        f = jnp.dot(h, wo_ref[:, ci * width:(ci + 1) * width], preferred_element_type=F32) * decay
        if ci % 2 == 1:
            f = f * not_first
        o_ref[ci] = f


def _hyena_filters(L, w1, b1, w2, b2, w3, b3, freq, w_out):
    emb = w1.shape[0]
    hid = w1.shape[1]
    t = jnp.linspace(0.0, 1.0, L, dtype=F32)[:, None]
    w = (2.0 * math.pi / L) * jnp.arange(L, dtype=F32)[:, None]
    bands = jnp.linspace(1e-4, HY_EMB_BANDS - 1, HY_EMB_BANDS, dtype=F32)
    z = jnp.concatenate([t, jnp.cos(bands * w), -jnp.sin(bands * w)], axis=-1)
    z = jnp.pad(z, ((0, 0), (0, LANES - emb)))
    w1p = jnp.pad(w1, ((0, LANES - emb), (0, 0)))
    deltas = jnp.abs(jnp.linspace(HY_MIN_DECAY, HY_MAX_DECAY, HY_WIDTH, dtype=F32)).reshape(1, HY_WIDTH)
    n_chunks = w_out.shape[1] // HY_WIDTH
    tl = min(512, L)
    full = lambda a: pl.BlockSpec(a.shape, lambda i: (0,) * a.ndim)
    b1, b2, b3 = (b.reshape(1, hid) for b in (b1, b2, b3))
    args = (z, w1p, b1, w2, b2, w3, b3, freq, w_out, deltas)
    return pl.pallas_call(
        functools.partial(_filter_kernel, n_chunks=n_chunks),
        grid=(L // tl,),
        in_specs=[pl.BlockSpec((tl, LANES), lambda i: (i, 0))] + [full(a) for a in args[1:]],
        out_specs=pl.BlockSpec((n_chunks, tl, HY_WIDTH), lambda i: (0, i, 0)),
        out_shape=jax.ShapeDtypeStruct((n_chunks, L, HY_WIDTH), F32),
        compiler_params=_cparams("parallel"),
        name="hy_filter",
    )(*args)


def _dft_tables(L):
    P = DFT_P
    R = L // P
    Q = 2 * R
    N = 2 * L
    nk1 = Q // 2 + 1
    mr = -(-2 * nk1 // SUBLANES) * SUBLANES
    n2 = np.arange(P)[:, None, None]
    k1 = np.arange(nk1)[None, :, None]
    n1 = np.arange(R)[None, None, :]
    ph = 2.0 * np.pi * ((k1 * (P * n1 + n2)) % N) / N
    f1 = np.zeros((P, mr, R))
    f1[:, 0:2 * nk1:2, :] = np.cos(ph)
    f1[:, 1:2 * nk1:2, :] = -np.sin(ph)
    ck = np.where((np.arange(nk1) == 0) | (np.arange(nk1) == Q // 2), 1.0, 2.0)[None, :, None] / N
    g = np.zeros((P, R, mr))
    g[:, :, 0:2 * nk1:2] = np.transpose(ck * np.cos(ph), (0, 2, 1))
    g[:, :, 1:2 * nk1:2] = np.transpose(-ck * np.sin(ph), (0, 2, 1))
    a = np.arange(P)
    th = 2.0 * np.pi * ((a[:, None] * a[None, :]) % P) / P
    tr, ti = np.cos(th), -np.sin(th)
    d_fwd = np.block([[tr, -ti], [ti, tr]])
    d_inv = np.block([[tr, ti], [-ti, tr]])
    as_bf = lambda m: jnp.asarray(m, dtype=F32).astype(BF16)
    return dict(f1=as_bf(f1), g=as_bf(g), d_fwd=as_bf(d_fwd), d_inv=as_bf(d_inv), nk1=nk1, mr=mr, R=R)


def _swap_major_sublane(x):
    return jnp.swapaxes(x, 0, 1)


def _fwd_dft(u3s, a3, f1_ref, dfwd_ref, nk1, post):
    R, P, _ = u3s[0].shape
    W = a3.shape[2]

    def step1(g, _):
        n0 = pl.multiple_of(g * N2_GROUP, N2_GROUP)
        x = [u3[:, pl.ds(n0, N2_GROUP), :] for u3 in u3s]
        xt = _swap_major_sublane(x[0] if len(x) == 1 else jnp.concatenate(x, axis=-1))
        a = jnp.stack([jnp.dot(f1_ref[n0 + r], xt[r].astype(BF16), preferred_element_type=F32)
                       for r in range(N2_GROUP)])
        a3[:, pl.ds(n0, N2_GROUP), :] = _swap_major_sublane(a)
        return 0

    lax.fori_loop(0, P // N2_GROUP, step1, 0, unroll=OUTER_UNROLL)

    group = max(MXU_WIDTH // W, 1)

    def inner(k1, n):
        a = a3[pl.ds(2 * k1, 2 * n)].astype(BF16)
        a = [a[2 * i:2 * i + 2].reshape(2 * P, W) for i in range(n)]
        post(k1, n, jnp.dot(dfwd_ref[...], a[0] if n == 1 else jnp.concatenate(a, axis=1),
                            preferred_element_type=F32))

    n_groups = nk1 // group

    def step3(gi, _):
        inner(gi * group, group)
        return 0

    lax.fori_loop(0, n_groups, step3, 0, unroll=_inner_unroll(n_groups))
    for k1 in range(n_groups * group, nk1):
        inner(k1, 1)


def _filtfft_kernel(hf_ref, hb_ref, f1_ref, dfwd_ref, o_ref, a_scr, *, nk1):
    P = DFT_P

    def combine(k1, n, x):
        assert n == 1
        xf, xb = x[:, 0:LANES], x[:, LANES:2 * LANES]
        o_ref[0, k1] = jnp.concatenate([xf[0:P] + xb[0:P], xf[P:2 * P] - xb[P:2 * P]], axis=0).astype(o_ref.dtype)

    _fwd_dft([hf_ref.at[0], hb_ref.at[0]], a_scr, f1_ref, dfwd_ref, nk1, combine)


def _filter_spectra(filt, tabs):
    n_f, L, C = filt.shape
    P = DFT_P
    nk1, mr, R = tabs["nk1"], tabs["mr"], tabs["R"]
    filt4 = filt.reshape(n_f, R, P, C)
    kern = functools.partial(_filtfft_kernel, nk1=nk1)
    return pl.pallas_call(
        kern,
        grid=(n_f // 2, C // LANES),
        in_specs=[
            pl.BlockSpec((1, R, P, LANES), lambda o, c: (2 * o, 0, 0, c)),
            pl.BlockSpec((1, R, P, LANES), lambda o, c: (2 * o + 1, 0, 0, c)),
            _resident((P, mr, R), lambda o, c: (0, 0, 0)),
            _resident((2 * P, 2 * P), lambda o, c: (0, 0)),
        ],
        out_specs=pl.BlockSpec((1, nk1, 2 * P, LANES), lambda o, c: (o, 0, 0, c)),
        out_shape=jax.ShapeDtypeStruct((n_f // 2, nk1, 2 * P, C), BF16),
        scratch_shapes=[pltpu.VMEM((mr, P, 2 * LANES), F32)],
        compiler_params=_cparams("parallel", "parallel"),
        name="hy_filtfft",
    )(filt4, filt4, tabs["f1"], tabs["d_fwd"])


def _short_conv_chunk(u_ref, r0, rows, L, w_ref, b_ref):
    halo = 16
    c = u_ref[pl.ds(r0, rows), :].astype(F32)
    p0 = pl.multiple_of(jnp.maximum(r0 - halo, 0), halo)
    n0 = pl.multiple_of(jnp.minimum(r0 + rows, L - halo), halo)
    prev = u_ref[pl.ds(p0, halo), :].astype(F32)[halo - 1:halo]
    nxt = u_ref[pl.ds(n0, halo), :].astype(F32)[0:1]
    prev = jnp.where(r0 > 0, prev, 0.0)
    nxt = jnp.where(r0 + rows < L, nxt, 0.0)
    row = lax.broadcasted_iota(jnp.int32, (rows, 1), 0)
    up = jnp.where(row == 0, prev, pltpu.roll(c, 1, axis=0))
    dn = jnp.where(row == rows - 1, nxt, pltpu.roll(c, rows - 1, axis=0))
    return up * w_ref[0:1, :] + c * w_ref[1:2, :] + dn * w_ref[2:3, :] + b_ref[...]


def _hyconv_kernel(u_ref, gt_ref, kf_ref, f1_ref, g_ref, dfwd_ref, dinv_ref, wu_ref, bu_ref,
                   wg_ref, bg_ref, bias_ref, o_ref, u_scr, a_scr, y_scr, *, nk1, conv_u, chunk):
    R, P, _ = u_scr.shape
    L = R * P
    n_chunks = L // chunk
    cr = chunk // P

    def load_u(i, _):
        r0 = pl.multiple_of(i * chunk, chunk)
        if conv_u:
            u = _short_conv_chunk(u_ref, r0, chunk, L, wu_ref, bu_ref)
        else:
            u = u_ref[pl.ds(r0, chunk), :].astype(F32)
        u_scr[pl.ds(i * cr, cr)] = u.reshape(cr, P, LANES)
        return 0

    lax.fori_loop(0, n_chunks, load_u, 0)

    def spectral(k1, n, x):
        ys = []
        for i in range(n):
            kf = kf_ref[0, k1 + i].astype(F32)
            xs = x[:, i * LANES:(i + 1) * LANES]
            xr, xi = xs[0:P], xs[P:2 * P]
            kr, ki = kf[0:P], kf[P:2 * P]
            ys.append(jnp.concatenate([xr * kr - xi * ki, xr * ki + xi * kr], axis=0).astype(BF16))
        b = jnp.dot(dinv_ref[...], ys[0] if n == 1 else jnp.concatenate(ys, axis=1),
                    preferred_element_type=F32)
        for i in range(n):
            a_scr[pl.ds(2 * (k1 + i), 2)] = b[:, i * LANES:(i + 1) * LANES].reshape(2, P, LANES)

    _fwd_dft([u_scr], a_scr, f1_ref, dfwd_ref, nk1, spectral)

    def inv_outer(g, _):
        n0 = pl.multiple_of(g * N2_GROUP, N2_GROUP)
        bt = _swap_major_sublane(a_scr[:, pl.ds(n0, N2_GROUP), :])
        y = jnp.stack([jnp.dot(g_ref[n0 + r], bt[r].astype(BF16), preferred_element_type=F32)
                       for r in range(N2_GROUP)])
        y_scr[:, pl.ds(n0, N2_GROUP), :] = _swap_major_sublane(y)
        return 0

    lax.fori_loop(0, P // N2_GROUP, inv_outer, 0, unroll=OUTER_UNROLL)

    def epilogue(i, _):
        r0 = pl.multiple_of(i * chunk, chunk)
        gate = _short_conv_chunk(gt_ref, r0, chunk, L, wg_ref, bg_ref)
        rows = pl.ds(i * cr, cr)
        y = y_scr[rows].reshape(chunk, LANES) + u_scr[rows].reshape(chunk, LANES) * bias_ref[...]
        o_ref[pl.ds(r0, chunk), :] = (gate * y).astype(o_ref.dtype)
        return 0

    lax.fori_loop(0, n_chunks, epilogue, 0)


def _hyena_conv(u_arr, u_col0, gate_arr, gate_col0, kf, order, conv_w, conv_b, bias, tabs, B, L, conv_u):
    P = DFT_P
    nk1, mr, R = tabs["nk1"], tabs["mr"], tabs["R"]
    C = HY_WIDTH
    n_ct = C // LANES
    chunk = min(512, L)
    kern = functools.partial(_hyconv_kernel, nk1=nk1, conv_u=conv_u, chunk=chunk)
    ucb, gcb = u_col0 // LANES, gate_col0 // LANES
    return pl.pallas_call(
        kern,
        grid=(n_ct, B),
        in_specs=[
            pl.BlockSpec((L, LANES), lambda c, b: (b, ucb + c)),
            pl.BlockSpec((L, LANES), lambda c, b: (b, gcb + c)),
            pl.BlockSpec((1, nk1, 2 * P, LANES), lambda c, b: (order, 0, 0, c)),
            _resident((P, mr, R), lambda c, b: (0, 0, 0)),
            _resident((P, R, mr), lambda c, b: (0, 0, 0)),
            _resident((2 * P, 2 * P), lambda c, b: (0, 0)),
            _resident((2 * P, 2 * P), lambda c, b: (0, 0)),
            pl.BlockSpec((3, LANES), lambda c, b: (0, ucb + c if conv_u else c)),
            pl.BlockSpec((1, LANES), lambda c, b: (0, ucb + c if conv_u else c)),
            pl.BlockSpec((3, LANES), lambda c, b: (0, gcb + c)),
            pl.BlockSpec((1, LANES), lambda c, b: (0, gcb + c)),
            pl.BlockSpec((1, LANES), lambda c, b: (0, c)),
        ],
        out_specs=pl.BlockSpec((L, LANES), lambda c, b: (b, c)),
        out_shape=jax.ShapeDtypeStruct((B * L, C), BF16),
        scratch_shapes=[
            pltpu.VMEM((R, P, LANES), F32),
            pltpu.VMEM((mr, P, LANES), F32),
            pltpu.VMEM((R, P, LANES), F32),
        ],
        compiler_params=_cparams("parallel", "arbitrary"),
        name=f"hy_conv{order}",
    )(u_arr, gate_arr, kf, tabs["f1"], tabs["g"], tabs["d_fwd"], tabs["d_inv"],
      conv_w, conv_b, conv_w, conv_b, bias[order].reshape(1, C))


def _attn_kernel(q_ref, k_ref, v_ref, lamv_ref, sg_ref, o_ref, vt_scr, m_scr, acc_scr,
                 *score_bufs, lam_init):
    n_kv, vrows, tk = vt_scr.shape
    dv = v_ref.shape[1]
    d = ATT_HEAD_DIM

    @pl.when(pl.program_id(2) == 0)
    def _():
        for c in range(n_kv):
            vt_scr[c, 0:dv, :] = v_ref[c * tk:(c + 1) * tk, :].astype(F32).T.astype(BF16)
            vt_scr[c, dv:vrows, :] = jnp.ones((vrows - dv, tk), BF16)

    qt = q_ref[...].astype(F32).T
    row = lax.broadcasted_iota(jnp.int32, (2 * d, 1), 0)
    qts = (jnp.where(row < d, qt, 0.0).astype(BF16), jnp.where(row >= d, qt, 0.0).astype(BF16))
    m_scr[...] = jnp.full(m_scr.shape, -jnp.inf, F32)
    acc_scr[...] = jnp.zeros(acc_scr.shape, F32)

    tq = q_ref.shape[0]

    def scores(j, s_ref, cm_ref, q0):
        kc = k_ref[pl.ds(pl.multiple_of(j * tk, tk), tk), :]
        qs = slice(q0, q0 + ATTN_QUERY_TILE)
        for comp in range(2):
            s = jnp.dot(kc, qts[comp][:, qs], preferred_element_type=F32)
            s_ref[comp, :, qs] = s
            cm_ref[comp, :, qs] = jnp.max(s, axis=0, keepdims=True)

    def accumulate(j, s_ref, cm_ref, q0):
        vt = vt_scr[j]
        qs = slice(q0, q0 + ATTN_QUERY_TILE)
        for comp in range(2):
            m_old = m_scr[comp, :, qs]
            m_new = jnp.maximum(m_old, cm_ref[comp, :, qs])
            p = jnp.exp2(s_ref[comp, :, qs] - m_new).astype(BF16)
            acc_scr[comp, :, qs] = (jnp.exp2(m_old - m_new) * acc_scr[comp, :, qs]
                                    + jnp.dot(vt, p, preferred_element_type=F32))
            m_scr[comp, :, qs] = m_new

    def step(score_job, acc_job):
        for q0 in range(0, tq, ATTN_QUERY_TILE):
            if score_job is not None:
                scores(*score_job, q0)
            if acc_job is not None:
                accumulate(*acc_job, q0)

    nbuf = len(score_bufs) // 2
    bufs = [(score_bufs[i], score_bufs[nbuf + i]) for i in range(nbuf)]
    depth = nbuf - 1
    for c in range(min(depth, n_kv)):
        step((c, *bufs[c % nbuf]), None)
    trips = max((n_kv - depth) // nbuf, 0)

    def body(jj, _):
        j = jj * nbuf
        for t in range(nbuf):
            step((j + t + depth, *bufs[(t + depth) % nbuf]), (j + t, *bufs[t]))
        return 0

    if trips > 0:
        lax.fori_loop(0, trips, body, 0, unroll=ATTN_TRIP_UNROLL)
    for c in range(trips * nbuf, n_kv):
        later = (c + depth, *bufs[(c + depth) % nbuf]) if c + depth < n_kv else None
        step(later, (c, *bufs[c % nbuf]))

    lv = lamv_ref[...]
    lam = (jnp.exp(jnp.sum(lv[0:1] * lv[1:2], axis=-1, keepdims=True))
           - jnp.exp(jnp.sum(lv[2:3] * lv[3:4], axis=-1, keepdims=True)) + lam_init)
    a1, a2 = acc_scr[0], acc_scr[1]
    o = a1[0:dv] / a1[dv:dv + 1] - lam * (a2[0:dv] / a2[dv:dv + 1])
    ms = jnp.mean(o * o, axis=0, keepdims=True)
    o = o * lax.rsqrt(ms + EPS) * sg_ref[...] * (1.0 - lam_init)
    o_ref[...] = o.T.astype(o_ref.dtype)


def _diff_attention(proj, q_col0, k_col0, v_col0, lam_vecs, subln_g, lam_init, B, L):
    tq = min(2048, L)
    tk = min(512, L)
    hw = ATT_V_DIM
    vrows = hw + 16
    qcb, kcb, vcb = q_col0 // hw, k_col0 // hw, v_col0 // hw
    nq = L // tq
    kern = functools.partial(_attn_kernel, lam_init=lam_init)
    return pl.pallas_call(
        kern,
        grid=(B, ATT_HEADS, nq),
        in_specs=[
            pl.BlockSpec((tq, hw), lambda b, h, i: (b * nq + i, qcb + h)),
            pl.BlockSpec((L, hw), lambda b, h, i: (b, kcb + h)),
            pl.BlockSpec((L, hw), lambda b, h, i: (b, vcb + h)),
            pl.BlockSpec((4, ATT_HEAD_DIM), lambda b, h, i: (0, 0)),
            pl.BlockSpec((hw, 1), lambda b, h, i: (0, 0)),
        ],
        out_specs=pl.BlockSpec((tq, hw), lambda b, h, i: (b * nq + i, h)),
        out_shape=jax.ShapeDtypeStruct((B * L, ATT_HEADS * hw), BF16),
        scratch_shapes=[
            pltpu.VMEM((L // tk, vrows, tk), BF16),
            pltpu.VMEM((2, 1, tq), F32),
            pltpu.VMEM((2, vrows, tq), F32),
        ] + [pltpu.VMEM((2, tk, tq), F32)] * ATTN_SCORE_BUFS + [pltpu.VMEM((2, 1, tq), F32)] * ATTN_SCORE_BUFS,
        compiler_params=_cparams("parallel", "parallel", "arbitrary"),
        name="diff_attn",
    )(proj, proj, proj, lam_vecs, subln_g.reshape(hw, 1))


def _merge_kernel(yh_ref, ya_ref, gh_ref, ga_ref, wph_ref, wpa_ref, wo_ref, x_ref, mod_ref, o_ref):
    mh = jnp.dot(yh_ref[...], wph_ref[...], preferred_element_type=F32)
    ma = jnp.dot(ya_ref[...], wpa_ref[...], preferred_element_type=F32)
    merged = (gh_ref[...].astype(F32) * mh + ga_ref[...].astype(F32) * ma).astype(BF16)
    r = jnp.dot(merged, wo_ref[...], preferred_element_type=F32)
    o_ref[...] = x_ref[...] + mod_ref[0, 2:3, :] * r


def _merge_out(y_hy, y_att, proj, gate_col0, wph, wpa, wo, x2, mod, seq_len):
    T, D = x2.shape
    tm = 512
    bps = seq_len // tm
    gcb = gate_col0 // D
    cw = y_hy.shape[1]
    return pl.pallas_call(
        _merge_kernel,
        grid=(T // tm,),
        in_specs=[
            pl.BlockSpec((tm, cw), lambda i: (i, 0)),
            pl.BlockSpec((tm, cw), lambda i: (i, 0)),
            pl.BlockSpec((tm, D), lambda i: (i, gcb)),
            pl.BlockSpec((tm, D), lambda i: (i, gcb + 1)),
            _resident((cw, D), lambda i: (0, 0)),
            _resident((cw, D), lambda i: (0, 0)),
            _resident((D, D), lambda i: (0, 0)),
            pl.BlockSpec((tm, D), lambda i: (i, 0)),
            pl.BlockSpec((1, N_MOD, D), lambda i: (i // bps, 0, 0)),
        ],
        out_specs=pl.BlockSpec((tm, D), lambda i: (i, 0)),
        out_shape=jax.ShapeDtypeStruct((T, D), F32),
        compiler_params=_cparams("parallel"),
        name="merge_out",
    )(y_hy, y_att, proj, proj, wph, wpa, wo, x2, mod)


def _ffn_kernel(x_ref, mod_ref, g_ref, wg_ref, wu_ref, wd_ref, o_ref, h_scr, acc_scr):
    j = pl.program_id(1)

    @pl.when(j == 0)
    def _():
        h = _modulated_norm(x_ref[...], g_ref[...], mod_ref[0, 3:4, :], mod_ref[0, 4:5, :])
        h_scr[...] = h.astype(BF16)
        acc_scr[...] = jnp.zeros_like(acc_scr)

    h = h_scr[...]
    th = wd_ref.shape[0]
    down = None
    for n0 in range(0, th, FFN_HIDDEN_TILE):
        ns = slice(n0, n0 + FFN_HIDDEN_TILE)
        a = jnp.dot(h, wg_ref[:, ns], preferred_element_type=F32)
        u = jnp.dot(h, wu_ref[:, ns], preferred_element_type=F32)
        f = (a * jax.nn.sigmoid(a) * u).astype(BF16)
        part = jnp.dot(f, wd_ref[ns, :], preferred_element_type=F32)
        down = part if down is None else down + part
    acc_scr[...] += down

    @pl.when(j == pl.num_programs(1) - 1)
    def _():
        o_ref[...] = x_ref[...] + mod_ref[0, 5:6, :] * acc_scr[...]


def _ffn(x2, mod, norm_g, wg, wu, wd, seq_len):
    T, D = x2.shape
    H = wg.shape[1]
    tm, th = 512, 512
    bps = seq_len // tm
    return pl.pallas_call(
        _ffn_kernel,
        grid=(T // tm, H // th),
        in_specs=[
            pl.BlockSpec((tm, D), lambda i, j: (i, 0)),
            pl.BlockSpec((1, N_MOD, D), lambda i, j: (i // bps, 0, 0)),
            pl.BlockSpec((1, D), lambda i, j: (0, 0)),
            pl.BlockSpec((D, th), lambda i, j: (0, j)),
            pl.BlockSpec((D, th), lambda i, j: (0, j)),
            pl.BlockSpec((th, D), lambda i, j: (j, 0)),
        ],
        out_specs=pl.BlockSpec((tm, D), lambda i, j: (i, 0)),
        out_shape=jax.ShapeDtypeStruct((T, D), F32),
        scratch_shapes=[pltpu.VMEM((tm, D), BF16), pltpu.VMEM((tm, D), F32)],
        compiler_params=_cparams("parallel", "arbitrary"),
        name="ffn",
    )(x2, mod, norm_g.reshape(1, D), wg, wu, wd)


def kernel(x, c, positions, w_ada, b_ada, norm1_g, w_in, hy_conv_w, hy_conv_b, hy_filt_w1, hy_filt_b1, hy_filt_w2, hy_filt_b2, hy_filt_w3, hy_filt_b3, hy_filt_freq, hy_filt_w_out, hy_bias, q_norm_g, k_norm_g, lam_q1, lam_k1, lam_q2, lam_k2, subln_g, w_proj_hy, w_proj_att, w_out, norm2_g, w_gate, w_up, w_down):
    B, L, D = x.shape
    T = B * L
    depth = w_ada.shape[0]
    s_hy = (HY_ORDER + 1) * HY_WIDTH
    qk_w = ATT_HEADS * 2 * ATT_HEAD_DIM
    q0, k0 = s_hy, s_hy + qk_w
    v0 = k0 + qk_w
    g0 = v0 + ATT_HEADS * ATT_V_DIM
    tabs = _dft_tables(L)
    pos_col = positions.astype(F32).reshape(T, 1)
    x2 = x.reshape(T, D)
    for l in range(depth):
        lam_init = 0.8 - 0.6 * math.exp(-0.3 * l)
        mod = _ada_mod(c, w_ada[l], b_ada[l])
        proj = _in_proj(x2, mod, norm1_g[l], w_in[l].astype(BF16), pos_col,
                        q_norm_g[l], k_norm_g[l], L)

        filt = _hyena_filters(L, hy_filt_w1[l], hy_filt_b1[l], hy_filt_w2[l], hy_filt_b2[l],
                              hy_filt_w3[l], hy_filt_b3[l], hy_filt_freq[l], hy_filt_w_out[l])
        kf = _filter_spectra(filt, tabs)
        cw, cb = hy_conv_w[l], hy_conv_b[l].reshape(1, s_hy)
        z = _hyena_conv(proj, 0, proj, HY_WIDTH, kf, 0, cw, cb, hy_bias[l], tabs, B, L, True)
        y_hy = _hyena_conv(z, 0, proj, 2 * HY_WIDTH, kf, 1, cw, cb, hy_bias[l], tabs, B, L, False)

        lam_vecs = jnp.stack([lam_q1[l], lam_k1[l], lam_q2[l], lam_k2[l]]).astype(F32)
        y_att = _diff_attention(proj, q0, k0, v0, lam_vecs, subln_g[l], lam_init, B, L)

        x2 = _merge_out(y_hy, y_att, proj, g0, w_proj_hy[l].astype(BF16), w_proj_att[l].astype(BF16),
                        w_out[l].astype(BF16), x2, mod, L)
        x2 = _ffn(x2, mod, norm2_g[l], w_gate[l].astype(BF16), w_up[l].astype(BF16),
                  w_down[l].astype(BF16), L)
    return x2.reshape(B, L, D)
```

```python
import functools
import math

import numpy as np
import jax
import jax.numpy as jnp
from jax import lax
from jax.experimental import pallas as pl
from jax.experimental.pallas import tpu as pltpu

F32 = jnp.float32
BF16 = jnp.bfloat16

ATT_HEADS = 8
ATT_HEAD_DIM = 64
ATT_V_DIM = 2 * ATT_HEAD_DIM
ROPE_DIM = ATT_HEAD_DIM // 4
ROPE_THETA = 500000.0
HY_WIDTH = 1024
HY_ORDER = 2
HY_EMB_BANDS = 16
HY_DECAY_TARGET = 1e-2
HY_MIN_DECAY = math.log(HY_DECAY_TARGET) / 0.3
HY_MAX_DECAY = math.log(HY_DECAY_TARGET) / 1.5
N_MOD = 6
EPS = 1e-6
LOG2_E = math.log2(math.e)

LANES = 128
SUBLANES = 8
MXU_WIDTH = 256
VMEM_LIMIT_BYTES = 56 * 1024 * 1024

ATTN_SCORE_BUFS = 2
ATTN_QUERY_TILE = MXU_WIDTH
ATTN_TRIP_UNROLL = 2
FFN_HIDDEN_TILE = MXU_WIDTH
INPROJ_ROW_CHUNK = 256
MERGE_ROW_CHUNK = 256
DFT_P = LANES
N2_GROUP = SUBLANES
OUTER_UNROLL = 4


def _inner_unroll(nk1):
    return next((u for u in (13, 8, 5, 4, 3, 2) if nk1 % u == 0), 1)


def _cparams(*sem):
    return pltpu.CompilerParams(dimension_semantics=sem, vmem_limit_bytes=VMEM_LIMIT_BYTES)


def _resident(block_shape, index_map):
    return pl.BlockSpec(block_shape, index_map, pipeline_mode=pl.Buffered(1))


def _ada_kernel(ct_ref, w_ref, b_ref, o_ref):
    w = w_ref[...]
    for b in range(ct_ref.shape[0]):
        cb = ct_ref[b]
        s = cb * jax.nn.sigmoid(cb)
        o_ref[b] = jnp.sum(w * s, axis=0, keepdims=True) + b_ref[...]


def _ada_mod(c, w_ada, b_ada):
    B, D = c.shape
    n = w_ada.shape[1]
    tn = 512
    out = pl.pallas_call(
        _ada_kernel,
        grid=(n // tn,),
        in_specs=[
            pl.BlockSpec((B, D, 1), lambda j: (0, 0, 0)),
            pl.BlockSpec((D, tn), lambda j: (0, j)),
            pl.BlockSpec((1, tn), lambda j: (0, j)),
        ],
        out_specs=pl.BlockSpec((B, 1, tn), lambda j: (0, 0, j)),
        out_shape=jax.ShapeDtypeStruct((B, 1, n), F32),
        compiler_params=_cparams("arbitrary"),
        name="ada_mod",
    )(c.reshape(B, D, 1), w_ada, b_ada.reshape(1, n))
    return out.reshape(B, N_MOD, D)


def _modulated_norm(x, g, shift, scale):
    ms = jnp.mean(x * x, axis=-1, keepdims=True)
    return (x * lax.rsqrt(ms + EPS) * g) * (1.0 + scale) + shift


def _inproj_kernel(x_ref, mod_ref, g_ref, w_ref, pos_ref, qg_ref, kg_ref, inv_ref, bd_ref,
                   o_ref, h_scr, rope_scr, *, jq, jk, jgate):
    j = pl.program_id(1)

    @pl.when(j == 0)
    def _():
        h = _modulated_norm(x_ref[...], g_ref[...], mod_ref[0, 0:1, :], mod_ref[0, 1:2, :])
        h_scr[...] = h.astype(BF16)
        ang = pos_ref[...] * inv_ref[...]
        cos, sin = jnp.cos(ang), jnp.sin(ang)
        lane = lax.broadcasted_iota(jnp.int32, (1, LANES), 1) % ATT_HEAD_DIM
        half = ROPE_DIM // 2
        rope_scr[0] = jnp.where(lane < ROPE_DIM, cos, 1.0)
        rope_scr[1] = jnp.where(lane < half, -sin, jnp.where(lane < ROPE_DIM, sin, 0.0))

    tm, tn = o_ref.shape

    def project(epilogue):
        for r in range(tm // INPROJ_ROW_CHUNK):
            rows = pl.ds(r * INPROJ_ROW_CHUNK, INPROJ_ROW_CHUNK)
            acc = jnp.dot(h_scr[rows, :], w_ref[...], preferred_element_type=F32)
            o_ref[rows, :] = epilogue(acc, rows).astype(o_ref.dtype)

    def qk_norm_rope(gain_ref, out_scale):
        def epilogue(acc, rows):
            sq = (acc * acc).astype(BF16)
            bw = bd_ref.shape[0]
            ms = jnp.concatenate(
                [jnp.dot(sq[:, g * bw:(g + 1) * bw], bd_ref[...], preferred_element_type=F32)
                 for g in range(tn // bw)], axis=1)
            y = acc * lax.rsqrt(ms + EPS) * gain_ref[...]
            reps = tn // LANES
            c_t = jnp.tile(rope_scr[0, rows, :], (1, reps))
            s_t = jnp.tile(rope_scr[1, rows, :], (1, reps))
            half = ROPE_DIM // 2
            lane = lax.broadcasted_iota(jnp.int32, (1, tn), 1) % ATT_HEAD_DIM
            y_pair = jnp.where(lane < half, pltpu.roll(y, tn - half, axis=1), pltpu.roll(y, half, axis=1))
            return (y * c_t + y_pair * s_t) * out_scale
        return epilogue

    @pl.when(j == jq)
    def _():
        project(qk_norm_rope(qg_ref, ATT_HEAD_DIM ** -0.5 * LOG2_E))

    @pl.when(j == jk)
    def _():
        project(qk_norm_rope(kg_ref, 1.0))

    @pl.when(j >= jgate)
    def _():
        project(lambda acc, rows: 0.5 * jnp.tanh(0.5 * acc) + 0.5)

    @pl.when((j != jq) & (j != jk) & (j < jgate))
    def _():
        project(lambda acc, rows: acc)


def _in_proj(x2, mod, norm_g, w_in_bf, pos_col, q_g, k_g, seq_len):
    T, D = x2.shape
    n = w_in_bf.shape[1]
    tm, tn = min(1024, seq_len), 1024
    blocks_per_seq = seq_len // tm
    s_hy = (HY_ORDER + 1) * HY_WIDTH
    qk_w = ATT_HEADS * 2 * ATT_HEAD_DIM
    assert tn == qk_w and s_hy % tn == 0
    jq = s_hy // tn
    jk = jq + 1
    jgate = jk + 2
    half = ROPE_DIM // 2
    inv = ROPE_THETA ** (-jnp.arange(half, dtype=F32) * (2.0 / ROPE_DIM))
    lane = np.arange(LANES) % ATT_HEAD_DIM
    inv_lane = jnp.where(lane < ROPE_DIM, inv[lane % half], 0.0).reshape(1, LANES).astype(F32)
    grp = np.arange(MXU_WIDTH) // ATT_HEAD_DIM
    bd = jnp.asarray((grp[:, None] == grp[None, :]).astype(np.float32) / ATT_HEAD_DIM, BF16)
    tile_g = lambda g: jnp.tile(g.reshape(1, ATT_HEAD_DIM), (1, tn // ATT_HEAD_DIM))
    kern = functools.partial(_inproj_kernel, jq=jq, jk=jk, jgate=jgate)
    return pl.pallas_call(
        kern,
        grid=(T // tm, n // tn),
        in_specs=[
            pl.BlockSpec((tm, D), lambda i, j: (i, 0)),
            pl.BlockSpec((1, N_MOD, D), lambda i, j: (i // blocks_per_seq, 0, 0)),
            pl.BlockSpec((1, D), lambda i, j: (0, 0)),
            pl.BlockSpec((D, tn), lambda i, j: (0, j)),
            pl.BlockSpec((tm, 1), lambda i, j: (i, 0)),
            pl.BlockSpec((1, tn), lambda i, j: (0, 0)),
            pl.BlockSpec((1, tn), lambda i, j: (0, 0)),
            pl.BlockSpec((1, LANES), lambda i, j: (0, 0)),
            pl.BlockSpec((MXU_WIDTH, MXU_WIDTH), lambda i, j: (0, 0)),
        ],
        out_specs=pl.BlockSpec((tm, tn), lambda i, j: (i, j)),
        out_shape=jax.ShapeDtypeStruct((T, n), BF16),
        scratch_shapes=[pltpu.VMEM((tm, D), BF16), pltpu.VMEM((2, tm, LANES), F32)],
        compiler_params=_cparams("parallel", "arbitrary"),
        name="in_proj",
    )(x2, mod, norm_g.reshape(1, D), w_in_bf, pos_col, tile_g(q_g), tile_g(k_g), inv_lane, bd)


def _filter_kernel(z_ref, w1_ref, b1_ref, w2_ref, b2_ref, w3_ref, b3_ref, fr_ref, wo_ref, dl_ref,
                   o_ref, *, n_chunks):
    tl = z_ref.shape[0]
    z = z_ref[...]
    h = jnp.sin(fr_ref[0:1, :] * (jnp.dot(z, w1_ref[...], preferred_element_type=F32) + b1_ref[...]))
    h = jnp.sin(fr_ref[1:2, :] * (jnp.dot(h, w2_ref[...], preferred_element_type=F32) + b2_ref[...]))
    h = jnp.sin(fr_ref[2:3, :] * (jnp.dot(h, w3_ref[...], preferred_element_type=F32) + b3_ref[...]))
    t = z[:, 0:1]
    decay = jnp.exp(-t * dl_ref[...])
    row = pl.program_id(0) * tl + lax.broadcasted_iota(jnp.int32, (tl, 1), 0)
    not_first = (row != 0).astype(F32)
    width = dl_ref.shape[1]
    for ci in range(n_chunks):
        f = jnp.dot(h, wo_ref[:, ci * width:(ci + 1) * width], preferred_element_type=F32) * decay
        if ci % 2 == 1:
            f = f * not_first
        o_ref[ci] = f


def _hyena_filters(L, w1, b1, w2, b2, w3, b3, freq, w_out):
    emb = w1.shape[0]
    hid = w1.shape[1]
    t = jnp.linspace(0.0, 1.0, L, dtype=F32)[:, None]
    w = (2.0 * math.pi / L) * jnp.arange(L, dtype=F32)[:, None]
    bands = jnp.linspace(1e-4, HY_EMB_BANDS - 1, HY_EMB_BANDS, dtype=F32)
    z = jnp.concatenate([t, jnp.cos(bands * w), -jnp.sin(bands * w)], axis=-1)
    z = jnp.pad(z, ((0, 0), (0, LANES - emb)))
    w1p = jnp.pad(w1, ((0, LANES - emb), (0, 0)))
    deltas = jnp.abs(jnp.linspace(HY_MIN_DECAY, HY_MAX_DECAY, HY_WIDTH, dtype=F32)).reshape(1, HY_WIDTH)
    n_chunks = w_out.shape[1] // HY_WIDTH
    tl = min(512, L)
    full = lambda a: pl.BlockSpec(a.shape, lambda i: (0,) * a.ndim)
    b1, b2, b3 = (b.reshape(1, hid) for b in (b1, b2, b3))
    args = (z, w1p, b1, w2, b2, w3, b3, freq, w_out, deltas)
    return pl.pallas_call(
        functools.partial(_filter_kernel, n_chunks=n_chunks),
        grid=(L // tl,),
        in_specs=[pl.BlockSpec((tl, LANES), lambda i: (i, 0))] + [full(a) for a in args[1:]],
        out_specs=pl.BlockSpec((n_chunks, tl, HY_WIDTH), lambda i: (0, i, 0)),
        out_shape=jax.ShapeDtypeStruct((n_chunks, L, HY_WIDTH), F32),
        compiler_params=_cparams("parallel"),
        name="hy_filter",
    )(*args)


def _dft_tables(L):
    P = DFT_P
    R = L // P
    Q = 2 * R
    N = 2 * L
    nk1 = Q // 2 + 1
    mr = -(-2 * nk1 // SUBLANES) * SUBLANES
    n2 = np.arange(P)[:, None, None]
    k1 = np.arange(nk1)[None, :, None]
    n1 = np.arange(R)[None, None, :]
    ph = 2.0 * np.pi * ((k1 * (P * n1 + n2)) % N) / N
    f1 = np.zeros((P, mr, R))
    f1[:, 0:2 * nk1:2, :] = np.cos(ph)
    f1[:, 1:2 * nk1:2, :] = -np.sin(ph)
    ck = np.where((np.arange(nk1) == 0) | (np.arange(nk1) == Q // 2), 1.0, 2.0)[None, :, None] / N
    g = np.zeros((P, R, mr))
    g[:, :, 0:2 * nk1:2] = np.transpose(ck * np.cos(ph), (0, 2, 1))
    g[:, :, 1:2 * nk1:2] = np.transpose(-ck * np.sin(ph), (0, 2, 1))
    a = np.arange(P)
    th = 2.0 * np.pi * ((a[:, None] * a[None, :]) % P) / P
    tr, ti = np.cos(th), -np.sin(th)
    d_fwd = np.block([[tr, -ti], [ti, tr]])
    d_inv = np.block([[tr, ti], [-ti, tr]])
    as_bf = lambda m: jnp.asarray(m, dtype=F32).astype(BF16)
    return dict(f1=as_bf(f1), g=as_bf(g), d_fwd=as_bf(d_fwd), d_inv=as_bf(d_inv), nk1=nk1, mr=mr, R=R)


def _swap_major_sublane(x):
    return jnp.swapaxes(x, 0, 1)


def _fwd_dft(u3s, a3, f1_ref, dfwd_ref, nk1, post):
    R, P, _ = u3s[0].shape
    W = a3.shape[2]

    def step1(g, _):
        n0 = pl.multiple_of(g * N2_GROUP, N2_GROUP)
        x = [u3[:, pl.ds(n0, N2_GROUP), :] for u3 in u3s]
        xt = _swap_major_sublane(x[0] if len(x) == 1 else jnp.concatenate(x, axis=-1))
        a = jnp.stack([jnp.dot(f1_ref[n0 + r], xt[r].astype(BF16), preferred_element_type=F32)
                       for r in range(N2_GROUP)])
        a3[:, pl.ds(n0, N2_GROUP), :] = _swap_major_sublane(a)
        return 0

    lax.fori_loop(0, P // N2_GROUP, step1, 0, unroll=OUTER_UNROLL)

    group = max(MXU_WIDTH // W, 1)

    def inner(k1, n):
        a = a3[pl.ds(2 * k1, 2 * n)].astype(BF16)
        a = [a[2 * i:2 * i + 2].reshape(2 * P, W) for i in range(n)]
        post(k1, n, jnp.dot(dfwd_ref[...], a[0] if n == 1 else jnp.concatenate(a, axis=1),
                            preferred_element_type=F32))

    n_groups = nk1 // group

    def step3(gi, _):
        inner(gi * group, group)
        return 0

    lax.fori_loop(0, n_groups, step3, 0, unroll=_inner_unroll(n_groups))
    for k1 in range(n_groups * group, nk1):
        inner(k1, 1)


def _filtfft_kernel(hf_ref, hb_ref, f1_ref, dfwd_ref, o_ref, a_scr, *, nk1):
    P = DFT_P

    def combine(k1, n, x):
        assert n == 1
        xf, xb = x[:, 0:LANES], x[:, LANES:2 * LANES]
        o_ref[0, k1] = jnp.concatenate([xf[0:P] + xb[0:P], xf[P:2 * P] - xb[P:2 * P]], axis=0).astype(o_ref.dtype)

    _fwd_dft([hf_ref.at[0], hb_ref.at[0]], a_scr, f1_ref, dfwd_ref, nk1, combine)


def _filter_spectra(filt, tabs):
    n_f, L, C = filt.shape
    P = DFT_P
    nk1, mr, R = tabs["nk1"], tabs["mr"], tabs["R"]
    filt4 = filt.reshape(n_f, R, P, C)
    kern = functools.partial(_filtfft_kernel, nk1=nk1)
    return pl.pallas_call(
        kern,
        grid=(n_f // 2, C // LANES),
        in_specs=[
            pl.BlockSpec((1, R, P, LANES), lambda o, c: (2 * o, 0, 0, c)),
            pl.BlockSpec((1, R, P, LANES), lambda o, c: (2 * o + 1, 0, 0, c)),
            _resident((P, mr, R), lambda o, c: (0, 0, 0)),
            _resident((2 * P, 2 * P), lambda o, c: (0, 0)),
        ],
        out_specs=pl.BlockSpec((1, nk1, 2 * P, LANES), lambda o, c: (o, 0, 0, c)),
        out_shape=jax.ShapeDtypeStruct((n_f // 2, nk1, 2 * P, C), BF16),
        scratch_shapes=[pltpu.VMEM((mr, P, 2 * LANES), F32)],
        compiler_params=_cparams("parallel", "parallel"),
        name="hy_filtfft",
    )(filt4, filt4, tabs["f1"], tabs["d_fwd"])


def _short_conv_chunk(u_ref, r0, rows, L, w_ref, b_ref):
    halo = 16
    c = u_ref[pl.ds(r0, rows), :].astype(F32)
    p0 = pl.multiple_of(jnp.maximum(r0 - halo, 0), halo)
    n0 = pl.multiple_of(jnp.minimum(r0 + rows, L - halo), halo)
    prev = u_ref[pl.ds(p0, halo), :].astype(F32)[halo - 1:halo]
    nxt = u_ref[pl.ds(n0, halo), :].astype(F32)[0:1]
    prev = jnp.where(r0 > 0, prev, 0.0)
    nxt = jnp.where(r0 + rows < L, nxt, 0.0)
    row = lax.broadcasted_iota(jnp.int32, (rows, 1), 0)
    up = jnp.where(row == 0, prev, pltpu.roll(c, 1, axis=0))
    dn = jnp.where(row == rows - 1, nxt, pltpu.roll(c, rows - 1, axis=0))
    return up * w_ref[0:1, :] + c * w_ref[1:2, :] + dn * w_ref[2:3, :] + b_ref[...]


def _hyconv_kernel(u_ref, gt_ref, kf_ref, f1_ref, g_ref, dfwd_ref, dinv_ref, wu_ref, bu_ref,
                   wg_ref, bg_ref, bias_ref, o_ref, u_scr, a_scr, y_scr, *, nk1, conv_u, chunk):
    R, P, _ = u_scr.shape
    L = R * P
    n_chunks = L // chunk
    cr = chunk // P

    def load_u(i, _):
        r0 = pl.multiple_of(i * chunk, chunk)
        if conv_u:
            u = _short_conv_chunk(u_ref, r0, chunk, L, wu_ref, bu_ref)
        else:
            u = u_ref[pl.ds(r0, chunk), :].astype(F32)
        u_scr[pl.ds(i * cr, cr)] = u.reshape(cr, P, LANES)
        return 0

    lax.fori_loop(0, n_chunks, load_u, 0)

    def spectral(k1, n, x):
        ys = []
        for i in range(n):
            kf = kf_ref[0, k1 + i].astype(F32)
            xs = x[:, i * LANES:(i + 1) * LANES]
            xr, xi = xs[0:P], xs[P:2 * P]
            kr, ki = kf[0:P], kf[P:2 * P]
            ys.append(jnp.concatenate([xr * kr - xi * ki, xr * ki + xi * kr], axis=0).astype(BF16))
        b = jnp.dot(dinv_ref[...], ys[0] if n == 1 else jnp.concatenate(ys, axis=1),
                    preferred_element_type=F32)
        for i in range(n):
            a_scr[pl.ds(2 * (k1 + i), 2)] = b[:, i * LANES:(i + 1) * LANES].reshape(2, P, LANES)

    _fwd_dft([u_scr], a_scr, f1_ref, dfwd_ref, nk1, spectral)

    def inv_outer(g, _):
        n0 = pl.multiple_of(g * N2_GROUP, N2_GROUP)
        bt = _swap_major_sublane(a_scr[:, pl.ds(n0, N2_GROUP), :])
        y = jnp.stack([jnp.dot(g_ref[n0 + r], bt[r].astype(BF16), preferred_element_type=F32)
                       for r in range(N2_GROUP)])
        y_scr[:, pl.ds(n0, N2_GROUP), :] = _swap_major_sublane(y)
        return 0

    lax.fori_loop(0, P // N2_GROUP, inv_outer, 0, unroll=OUTER_UNROLL)

    def epilogue(i, _):
        r0 = pl.multiple_of(i * chunk, chunk)
        gate = _short_conv_chunk(gt_ref, r0, chunk, L, wg_ref, bg_ref)
        rows = pl.ds(i * cr, cr)
        y = y_scr[rows].reshape(chunk, LANES) + u_scr[rows].reshape(chunk, LANES) * bias_ref[...]
        o_ref[pl.ds(r0, chunk), :] = (gate * y).astype(o_ref.dtype)
        return 0

    lax.fori_loop(0, n_chunks, epilogue, 0)


def _hyena_conv(u_arr, u_col0, gate_arr, gate_col0, kf, order, conv_w, conv_b, bias, tabs, B, L, conv_u):
    P = DFT_P
    nk1, mr, R = tabs["nk1"], tabs["mr"], tabs["R"]
    C = HY_WIDTH
    n_ct = C // LANES
    chunk = min(512, L)
    kern = functools.partial(_hyconv_kernel, nk1=nk1, conv_u=conv_u, chunk=chunk)
    ucb, gcb = u_col0 // LANES, gate_col0 // LANES
    return pl.pallas_call(
        kern,
        grid=(n_ct, B),
        in_specs=[
            pl.BlockSpec((L, LANES), lambda c, b: (b, ucb + c)),
            pl.BlockSpec((L, LANES), lambda c, b: (b, gcb + c)),
            pl.BlockSpec((1, nk1, 2 * P, LANES), lambda c, b: (order, 0, 0, c)),
            _resident((P, mr, R), lambda c, b: (0, 0, 0)),
            _resident((P, R, mr), lambda c, b: (0, 0, 0)),
            _resident((2 * P, 2 * P), lambda c, b: (0, 0)),
            _resident((2 * P, 2 * P), lambda c, b: (0, 0)),
            pl.BlockSpec((3, LANES), lambda c, b: (0, ucb + c if conv_u else c)),
            pl.BlockSpec((1, LANES), lambda c, b: (0, ucb + c if conv_u else c)),
            pl.BlockSpec((3, LANES), lambda c, b: (0, gcb + c)),
            pl.BlockSpec((1, LANES), lambda c, b: (0, gcb + c)),
            pl.BlockSpec((1, LANES), lambda c, b: (0, c)),
        ],
        out_specs=pl.BlockSpec((L, LANES), lambda c, b: (b, c)),
        out_shape=jax.ShapeDtypeStruct((B * L, C), BF16),
        scratch_shapes=[
            pltpu.VMEM((R, P, LANES), F32),
            pltpu.VMEM((mr, P, LANES), F32),
            pltpu.VMEM((R, P, LANES), F32),
        ],
        compiler_params=_cparams("parallel", "arbitrary"),
        name=f"hy_conv{order}",
    )(u_arr, gate_arr, kf, tabs["f1"], tabs["g"], tabs["d_fwd"], tabs["d_inv"],
      conv_w, conv_b, conv_w, conv_b, bias[order].reshape(1, C))


def _attn_kernel(q_ref, k_ref, v_ref, lamv_ref, sg_ref, o_ref, vt_scr, m_scr, acc_scr,
                 *score_bufs, lam_init):
    n_kv, vrows, tk = vt_scr.shape
    dv = v_ref.shape[1]
    d = ATT_HEAD_DIM

    @pl.when(pl.program_id(2) == 0)
    def _():
        for c in range(n_kv):
            vt_scr[c, 0:dv, :] = v_ref[c * tk:(c + 1) * tk, :].astype(F32).T.astype(BF16)
            vt_scr[c, dv:vrows, :] = jnp.ones((vrows - dv, tk), BF16)

    qt = q_ref[...].astype(F32).T
    row = lax.broadcasted_iota(jnp.int32, (2 * d, 1), 0)
    qts = (jnp.where(row < d, qt, 0.0).astype(BF16), jnp.where(row >= d, qt, 0.0).astype(BF16))
    m_scr[...] = jnp.full(m_scr.shape, -jnp.inf, F32)
    acc_scr[...] = jnp.zeros(acc_scr.shape, F32)

    tq = q_ref.shape[0]

    def scores(j, s_ref, cm_ref, q0):
        kc = k_ref[pl.ds(pl.multiple_of(j * tk, tk), tk), :]
        qs = slice(q0, q0 + ATTN_QUERY_TILE)
        for comp in range(2):
            s = jnp.dot(kc, qts[comp][:, qs], preferred_element_type=F32)
            s_ref[comp, :, qs] = s
            cm_ref[comp, :, qs] = jnp.max(s, axis=0, keepdims=True)

    def accumulate(j, s_ref, cm_ref, q0):
        vt = vt_scr[j]
        qs = slice(q0, q0 + ATTN_QUERY_TILE)
        for comp in range(2):
            m_old = m_scr[comp, :, qs]
            m_new = jnp.maximum(m_old, cm_ref[comp, :, qs])
            p = jnp.exp2(s_ref[comp, :, qs] - m_new).astype(BF16)
            acc_scr[comp, :, qs] = (jnp.exp2(m_old - m_new) * acc_scr[comp, :, qs]
                                    + jnp.dot(vt, p, preferred_element_type=F32))
            m_scr[comp, :, qs] = m_new

    def step(score_job, acc_job):
        for q0 in range(0, tq, ATTN_QUERY_TILE):
            if score_job is not None:
                scores(*score_job, q0)
            if acc_job is not None:
                accumulate(*acc_job, q0)

    nbuf = len(score_bufs) // 2
    bufs = [(score_bufs[i], score_bufs[nbuf + i]) for i in range(nbuf)]
    depth = nbuf - 1
    for c in range(min(depth, n_kv)):
        step((c, *bufs[c % nbuf]), None)
    trips = max((n_kv - depth) // nbuf, 0)

    def body(jj, _):
        j = jj * nbuf
        for t in range(nbuf):
            step((j + t + depth, *bufs[(t + depth) % nbuf]), (j + t, *bufs[t]))
        return 0

    if trips > 0:
        lax.fori_loop(0, trips, body, 0, unroll=ATTN_TRIP_UNROLL)
    for c in range(trips * nbuf, n_kv):
        later = (c + depth, *bufs[(c + depth) % nbuf]) if c + depth < n_kv else None
        step(later, (c, *bufs[c % nbuf]))

    lv = lamv_ref[...]
    lam = (jnp.exp(jnp.sum(lv[0:1] * lv[1:2], axis=-1, keepdims=True))
           - jnp.exp(jnp.sum(lv[2:3] * lv[3:4], axis=-1, keepdims=True)) + lam_init)
    a1, a2 = acc_scr[0], acc_scr[1]
    o = a1[0:dv] / a1[dv:dv + 1] - lam * (a2[0:dv] / a2[dv:dv + 1])
    ms = jnp.mean(o * o, axis=0, keepdims=True)
    o = o * lax.rsqrt(ms + EPS) * sg_ref[...] * (1.0 - lam_init)
    o_ref[...] = o.T.astype(o_ref.dtype)


def _diff_attention(proj, q_col0, k_col0, v_col0, lam_vecs, subln_g, lam_init, B, L):
    tq = min(2048, L)
    tk = min(512, L)
    hw = ATT_V_DIM
    vrows = hw + 16
    qcb, kcb, vcb = q_col0 // hw, k_col0 // hw, v_col0 // hw
    nq = L // tq
    kern = functools.partial(_attn_kernel, lam_init=lam_init)
    return pl.pallas_call(
        kern,
        grid=(B, ATT_HEADS, nq),
        in_specs=[
            pl.BlockSpec((tq, hw), lambda b, h, i: (b * nq + i, qcb + h)),
            pl.BlockSpec((L, hw), lambda b, h, i: (b, kcb + h)),
            pl.BlockSpec((L, hw), lambda b, h, i: (b, vcb + h)),
            pl.BlockSpec((4, ATT_HEAD_DIM), lambda b, h, i: (0, 0)),
            pl.BlockSpec((hw, 1), lambda b, h, i: (0, 0)),
        ],
        out_specs=pl.BlockSpec((tq, hw), lambda b, h, i: (b * nq + i, h)),
        out_shape=jax.ShapeDtypeStruct((B * L, ATT_HEADS * hw), BF16),
        scratch_shapes=[
            pltpu.VMEM((L // tk, vrows, tk), BF16),
            pltpu.VMEM((2, 1, tq), F32),
            pltpu.VMEM((2, vrows, tq), F32),
        ] + [pltpu.VMEM((2, tk, tq), F32)] * ATTN_SCORE_BUFS + [pltpu.VMEM((2, 1, tq), F32)] * ATTN_SCORE_BUFS,
        compiler_params=_cparams("parallel", "parallel", "arbitrary"),
        name="diff_attn",
    )(proj, proj, proj, lam_vecs, subln_g.reshape(hw, 1))


def _merge_kernel(yh_ref, ya_ref, gh_ref, ga_ref, wph_ref, wpa_ref, wo_ref, x_ref, mod_ref, g2_ref,
                  o_ref, h2_ref):
    tm = o_ref.shape[0]
    for r in range(tm // MERGE_ROW_CHUNK):
        rows = pl.ds(r * MERGE_ROW_CHUNK, MERGE_ROW_CHUNK)
        mh = jnp.dot(yh_ref[rows, :], wph_ref[...], preferred_element_type=F32)
        ma = jnp.dot(ya_ref[rows, :], wpa_ref[...], preferred_element_type=F32)
        merged = (gh_ref[rows, :].astype(F32) * mh + ga_ref[rows, :].astype(F32) * ma).astype(BF16)
        x1 = x_ref[rows, :] + mod_ref[0, 2:3, :] * jnp.dot(merged, wo_ref[...], preferred_element_type=F32)
        o_ref[rows, :] = x1
        h2 = _modulated_norm(x1, g2_ref[...], mod_ref[0, 3:4, :], mod_ref[0, 4:5, :])
        h2_ref[rows, :] = h2.astype(h2_ref.dtype)


def _merge_out(y_hy, y_att, proj, gate_col0, wph, wpa, wo, x2, mod, norm2_g, seq_len):
    T, D = x2.shape
    tm = 512
    bps = seq_len // tm
    gcb = gate_col0 // D
    cw = y_hy.shape[1]
    return pl.pallas_call(
        _merge_kernel,
        grid=(T // tm,),
        in_specs=[
            pl.BlockSpec((tm, cw), lambda i: (i, 0)),
            pl.BlockSpec((tm, cw), lambda i: (i, 0)),
            pl.BlockSpec((tm, D), lambda i: (i, gcb)),
            pl.BlockSpec((tm, D), lambda i: (i, gcb + 1)),
            _resident((cw, D), lambda i: (0, 0)),
            _resident((cw, D), lambda i: (0, 0)),
            _resident((D, D), lambda i: (0, 0)),
            pl.BlockSpec((tm, D), lambda i: (i, 0)),
            pl.BlockSpec((1, N_MOD, D), lambda i: (i // bps, 0, 0)),
            pl.BlockSpec((1, D), lambda i: (0, 0)),
        ],
        out_specs=[pl.BlockSpec((tm, D), lambda i: (i, 0)), pl.BlockSpec((tm, D), lambda i: (i, 0))],
        out_shape=[jax.ShapeDtypeStruct((T, D), F32), jax.ShapeDtypeStruct((T, D), BF16)],
        compiler_params=_cparams("parallel"),
        name="merge_out",
    )(y_hy, y_att, proj, proj, wph, wpa, wo, x2, mod, norm2_g.reshape(1, D))


def _ffn_kernel(x_ref, h_ref, mod_ref, wg_ref, wu_ref, wd_ref, o_ref, acc_scr):
    j = pl.program_id(1)

    @pl.when(j == 0)
    def _():
        acc_scr[...] = jnp.zeros_like(acc_scr)

    h = h_ref[...]
    th = wd_ref.shape[0]
    down = None
    for n0 in range(0, th, FFN_HIDDEN_TILE):
        ns = slice(n0, n0 + FFN_HIDDEN_TILE)
        a = jnp.dot(h, wg_ref[:, ns], preferred_element_type=F32)
        u = jnp.dot(h, wu_ref[:, ns], preferred_element_type=F32)
        f = (a * jax.nn.sigmoid(a) * u).astype(BF16)
        part = jnp.dot(f, wd_ref[ns, :], preferred_element_type=F32)
        down = part if down is None else down + part
    acc_scr[...] += down

    @pl.when(j == pl.num_programs(1) - 1)
    def _():
        o_ref[...] = x_ref[...] + mod_ref[0, 5:6, :] * acc_scr[...]


def _ffn(x2, h2, mod, wg, wu, wd, seq_len):
    T, D = x2.shape
    H = wg.shape[1]
    tm, th = 512, 512
    bps = seq_len // tm
    return pl.pallas_call(
        _ffn_kernel,
        grid=(T // tm, H // th),
        in_specs=[
            pl.BlockSpec((tm, D), lambda i, j: (i, 0)),
            pl.BlockSpec((tm, D), lambda i, j: (i, 0)),
            pl.BlockSpec((1, N_MOD, D), lambda i, j: (i // bps, 0, 0)),
            pl.BlockSpec((D, th), lambda i, j: (0, j)),
            pl.BlockSpec((D, th), lambda i, j: (0, j)),
            pl.BlockSpec((th, D), lambda i, j: (j, 0)),
        ],
        out_specs=pl.BlockSpec((tm, D), lambda i, j: (i, 0)),
        out_shape=jax.ShapeDtypeStruct((T, D), F32),
        scratch_shapes=[pltpu.VMEM((tm, D), F32)],
        compiler_params=_cparams("parallel", "arbitrary"),
        name="ffn",
    )(x2, h2, mod, wg, wu, wd)


def kernel(x, c, positions, w_ada, b_ada, norm1_g, w_in, hy_conv_w, hy_conv_b, hy_filt_w1, hy_filt_b1, hy_filt_w2, hy_filt_b2, hy_filt_w3, hy_filt_b3, hy_filt_freq, hy_filt_w_out, hy_bias, q_norm_g, k_norm_g, lam_q1, lam_k1, lam_q2, lam_k2, subln_g, w_proj_hy, w_proj_att, w_out, norm2_g, w_gate, w_up, w_down):
    B, L, D = x.shape
    T = B * L
    depth = w_ada.shape[0]
    s_hy = (HY_ORDER + 1) * HY_WIDTH
    qk_w = ATT_HEADS * 2 * ATT_HEAD_DIM
    q0, k0 = s_hy, s_hy + qk_w
    v0 = k0 + qk_w
    g0 = v0 + ATT_HEADS * ATT_V_DIM
    tabs = _dft_tables(L)
    pos_col = positions.astype(F32).reshape(T, 1)
    x2 = x.reshape(T, D)
    for l in range(depth):
        lam_init = 0.8 - 0.6 * math.exp(-0.3 * l)
        mod = _ada_mod(c, w_ada[l], b_ada[l])
        proj = _in_proj(x2, mod, norm1_g[l], w_in[l].astype(BF16), pos_col,
                        q_norm_g[l], k_norm_g[l], L)

        filt = _hyena_filters(L, hy_filt_w1[l], hy_filt_b1[l], hy_filt_w2[l], hy_filt_b2[l],
                              hy_filt_w3[l], hy_filt_b3[l], hy_filt_freq[l], hy_filt_w_out[l])
        kf = _filter_spectra(filt, tabs)
        cw, cb = hy_conv_w[l], hy_conv_b[l].reshape(1, s_hy)
        z = _hyena_conv(proj, 0, proj, HY_WIDTH, kf, 0, cw, cb, hy_bias[l], tabs, B, L, True)
        y_hy = _hyena_conv(z, 0, proj, 2 * HY_WIDTH, kf, 1, cw, cb, hy_bias[l], tabs, B, L, False)

        lam_vecs = jnp.stack([lam_q1[l], lam_k1[l], lam_q2[l], lam_k2[l]]).astype(F32)
        y_att = _diff_attention(proj, q0, k0, v0, lam_vecs, subln_g[l], lam_init, B, L)

        x2, h2 = _merge_out(y_hy, y_att, proj, g0, w_proj_hy[l].astype(BF16), w_proj_att[l].astype(BF16),
                            w_out[l].astype(BF16), x2, mod, norm2_g[l], L)
        x2 = _ffn(x2, h2, mod, w_gate[l].astype(BF16), w_up[l].astype(BF16), w_down[l].astype(BF16), L)
    return x2.reshape(B, L, D)
```

```python
import functools
import math

import numpy as np
import jax
import jax.numpy as jnp
from jax import lax
from jax.experimental import pallas as pl
from jax.experimental.pallas import tpu as pltpu

F32 = jnp.float32
BF16 = jnp.bfloat16

ATT_HEADS = 8
ATT_HEAD_DIM = 64
ATT_V_DIM = 2 * ATT_HEAD_DIM
ROPE_DIM = ATT_HEAD_DIM // 4
ROPE_THETA = 500000.0
HY_WIDTH = 1024
HY_ORDER = 2
HY_EMB_BANDS = 16
HY_DECAY_TARGET = 1e-2
HY_MIN_DECAY = math.log(HY_DECAY_TARGET) / 0.3
HY_MAX_DECAY = math.log(HY_DECAY_TARGET) / 1.5
N_MOD = 6
EPS = 1e-6
LOG2_E = math.log2(math.e)

LANES = 128
SUBLANES = 8
MXU_WIDTH = 256
VMEM_LIMIT_BYTES = 56 * 1024 * 1024

ATTN_SCORE_BUFS = 2
ATTN_QUERY_TILE = MXU_WIDTH
ATTN_TRIP_UNROLL = 2
FFN_HIDDEN_TILE = MXU_WIDTH
FFN_ROW_CHUNK = 256
INPROJ_ROW_CHUNK = 256
DFT_P = LANES
N2_GROUP = SUBLANES
OUTER_UNROLL = 4


def _inner_unroll(nk1):
    return next((u for u in (13, 8, 5, 4, 3, 2) if nk1 % u == 0), 1)


def _cparams(*sem):
    return pltpu.CompilerParams(dimension_semantics=sem, vmem_limit_bytes=VMEM_LIMIT_BYTES)


def _resident(block_shape, index_map):
    return pl.BlockSpec(block_shape, index_map, pipeline_mode=pl.Buffered(1))


def _ada_kernel(ct_ref, w_ref, b_ref, o_ref):
    w = w_ref[...]
    for b in range(ct_ref.shape[0]):
        cb = ct_ref[b]
        s = cb * jax.nn.sigmoid(cb)
        o_ref[b] = jnp.sum(w * s, axis=0, keepdims=True) + b_ref[...]


def _ada_mod(c, w_ada, b_ada):
    B, D = c.shape
    n = w_ada.shape[1]
    tn = 512
    out = pl.pallas_call(
        _ada_kernel,
        grid=(n // tn,),
        in_specs=[
            pl.BlockSpec((B, D, 1), lambda j: (0, 0, 0)),
            pl.BlockSpec((D, tn), lambda j: (0, j)),
            pl.BlockSpec((1, tn), lambda j: (0, j)),
        ],
        out_specs=pl.BlockSpec((B, 1, tn), lambda j: (0, 0, j)),
        out_shape=jax.ShapeDtypeStruct((B, 1, n), F32),
        compiler_params=_cparams("arbitrary"),
        name="ada_mod",
    )(c.reshape(B, D, 1), w_ada, b_ada.reshape(1, n))
    return out.reshape(B, N_MOD, D)


def _modulated_norm(x, g, shift, scale):
    ms = jnp.mean(x * x, axis=-1, keepdims=True)
    return (x * lax.rsqrt(ms + EPS) * g) * (1.0 + scale) + shift


def _inproj_kernel(x_ref, mod_ref, g_ref, w_ref, pos_ref, qg_ref, kg_ref, inv_ref, bd_ref,
                   o_ref, h_scr, rope_scr, *, jq, jk, jgate):
    j = pl.program_id(1)

    tm, tn = o_ref.shape

    @pl.when(j == 0)
    def _():
        for r in range(tm // INPROJ_ROW_CHUNK):
            rows = pl.ds(r * INPROJ_ROW_CHUNK, INPROJ_ROW_CHUNK)
            h = _modulated_norm(x_ref[rows, :], g_ref[...], mod_ref[0, 0:1, :], mod_ref[0, 1:2, :]).astype(BF16)
            h_scr[rows, :] = h
            o_ref[rows, :] = jnp.dot(h, w_ref[...], preferred_element_type=F32).astype(o_ref.dtype)
        ang = pos_ref[...] * inv_ref[...]
        cos, sin = jnp.cos(ang), jnp.sin(ang)
        lane = lax.broadcasted_iota(jnp.int32, (1, LANES), 1) % ATT_HEAD_DIM
        half = ROPE_DIM // 2
        rope_scr[0] = jnp.where(lane < ROPE_DIM, cos, 1.0)
        rope_scr[1] = jnp.where(lane < half, -sin, jnp.where(lane < ROPE_DIM, sin, 0.0))

    def project(epilogue):
        for r in range(tm // INPROJ_ROW_CHUNK):
            rows = pl.ds(r * INPROJ_ROW_CHUNK, INPROJ_ROW_CHUNK)
            acc = jnp.dot(h_scr[rows, :], w_ref[...], preferred_element_type=F32)
            o_ref[rows, :] = epilogue(acc, rows).astype(o_ref.dtype)

    def qk_norm_rope(gain_ref, out_scale):
        def epilogue(acc, rows):
            sq = (acc * acc).astype(BF16)
            bw = bd_ref.shape[0]
            ms = jnp.concatenate(
                [jnp.dot(sq[:, g * bw:(g + 1) * bw], bd_ref[...], preferred_element_type=F32)
                 for g in range(tn // bw)], axis=1)
            y = acc * lax.rsqrt(ms + EPS) * gain_ref[...]
            reps = tn // LANES
            c_t = jnp.tile(rope_scr[0, rows, :], (1, reps))
            s_t = jnp.tile(rope_scr[1, rows, :], (1, reps))
            half = ROPE_DIM // 2
            lane = lax.broadcasted_iota(jnp.int32, (1, tn), 1) % ATT_HEAD_DIM
            y_pair = jnp.where(lane < half, pltpu.roll(y, tn - half, axis=1), pltpu.roll(y, half, axis=1))
            return (y * c_t + y_pair * s_t) * out_scale
        return epilogue

    @pl.when(j == jq)
    def _():
        project(qk_norm_rope(qg_ref, ATT_HEAD_DIM ** -0.5 * LOG2_E))

    @pl.when(j == jk)
    def _():
        project(qk_norm_rope(kg_ref, 1.0))

    @pl.when(j >= jgate)
    def _():
        project(lambda acc, rows: 0.5 * jnp.tanh(0.5 * acc) + 0.5)

    @pl.when((j != 0) & (j != jq) & (j != jk) & (j < jgate))
    def _():
        project(lambda acc, rows: acc)


def _in_proj(x2, mod, norm_g, w_in_bf, pos_col, q_g, k_g, seq_len):
    T, D = x2.shape
    n = w_in_bf.shape[1]
    tm, tn = min(1024, seq_len), 1024
    blocks_per_seq = seq_len // tm
    s_hy = (HY_ORDER + 1) * HY_WIDTH
    qk_w = ATT_HEADS * 2 * ATT_HEAD_DIM
    assert tn == qk_w and s_hy % tn == 0 and s_hy >= tn
    jq = s_hy // tn
    jk = jq + 1
    jgate = jk + 2
    half = ROPE_DIM // 2
    inv = ROPE_THETA ** (-jnp.arange(half, dtype=F32) * (2.0 / ROPE_DIM))
    lane = np.arange(LANES) % ATT_HEAD_DIM
    inv_lane = jnp.where(lane < ROPE_DIM, inv[lane % half], 0.0).reshape(1, LANES).astype(F32)
    grp = np.arange(MXU_WIDTH) // ATT_HEAD_DIM
    bd = jnp.asarray((grp[:, None] == grp[None, :]).astype(np.float32) / ATT_HEAD_DIM, BF16)
    tile_g = lambda g: jnp.tile(g.reshape(1, ATT_HEAD_DIM), (1, tn // ATT_HEAD_DIM))
    kern = functools.partial(_inproj_kernel, jq=jq, jk=jk, jgate=jgate)
    return pl.pallas_call(
        kern,
        grid=(T // tm, n // tn),
        in_specs=[
            pl.BlockSpec((tm, D), lambda i, j: (i, 0)),
            pl.BlockSpec((1, N_MOD, D), lambda i, j: (i // blocks_per_seq, 0, 0)),
            pl.BlockSpec((1, D), lambda i, j: (0, 0)),
            pl.BlockSpec((D, tn), lambda i, j: (0, j)),
            pl.BlockSpec((tm, 1), lambda i, j: (i, 0)),
            pl.BlockSpec((1, tn), lambda i, j: (0, 0)),
            pl.BlockSpec((1, tn), lambda i, j: (0, 0)),
            pl.BlockSpec((1, LANES), lambda i, j: (0, 0)),
            pl.BlockSpec((MXU_WIDTH, MXU_WIDTH), lambda i, j: (0, 0)),
        ],
        out_specs=pl.BlockSpec((tm, tn), lambda i, j: (i, j)),
        out_shape=jax.ShapeDtypeStruct((T, n), BF16),
        scratch_shapes=[pltpu.VMEM((tm, D), BF16), pltpu.VMEM((2, tm, LANES), F32)],
        compiler_params=_cparams("parallel", "arbitrary"),
        name="in_proj",
    )(x2, mod, norm_g.reshape(1, D), w_in_bf, pos_col, tile_g(q_g), tile_g(k_g), inv_lane, bd)


def _filter_kernel(z_ref, w1_ref, b1_ref, w2_ref, b2_ref, w3_ref, b3_ref, fr_ref, wo_ref, dl_ref,
                   o_ref, *, n_chunks):
    tl = z_ref.shape[0]
    z = z_ref[...]
    h = jnp.sin(fr_ref[0:1, :] * (jnp.dot(z, w1_ref[...], preferred_element_type=F32) + b1_ref[...]))
    h = jnp.sin(fr_ref[1:2, :] * (jnp.dot(h, w2_ref[...], preferred_element_type=F32) + b2_ref[...]))
    h = jnp.sin(fr_ref[2:3, :] * (jnp.dot(h, w3_ref[...], preferred_element_type=F32) + b3_ref[...]))
    t = z[:, 0:1]
    decay = jnp.exp(-t * dl_ref[...])
    row = pl.program_id(0) * tl + lax.broadcasted_iota(jnp.int32, (tl, 1), 0)
    not_first = (row != 0).astype(F32)
    width = dl_ref.shape[1]
    for ci in range(n_chunks):
        f = jnp.dot(h, wo_ref[:, ci * width:(ci + 1) * width], preferred_element_type=F32) * decay
        if ci % 2 == 1:
            f = f * not_first
        o_ref[ci] = f


def _hyena_filters(L, w1, b1, w2, b2, w3, b3, freq, w_out):
    emb = w1.shape[0]
    hid = w1.shape[1]
    t = jnp.linspace(0.0, 1.0, L, dtype=F32)[:, None]
    w = (2.0 * math.pi / L) * jnp.arange(L, dtype=F32)[:, None]
    bands = jnp.linspace(1e-4, HY_EMB_BANDS - 1, HY_EMB_BANDS, dtype=F32)
    z = jnp.concatenate([t, jnp.cos(bands * w), -jnp.sin(bands * w)], axis=-1)
    z = jnp.pad(z, ((0, 0), (0, LANES - emb)))
    w1p = jnp.pad(w1, ((0, LANES - emb), (0, 0)))
    deltas = jnp.abs(jnp.linspace(HY_MIN_DECAY, HY_MAX_DECAY, HY_WIDTH, dtype=F32)).reshape(1, HY_WIDTH)
    n_chunks = w_out.shape[1] // HY_WIDTH
    tl = min(512, L)
    full = lambda a: pl.BlockSpec(a.shape, lambda i: (0,) * a.ndim)
    b1, b2, b3 = (b.reshape(1, hid) for b in (b1, b2, b3))
    args = (z, w1p, b1, w2, b2, w3, b3, freq, w_out, deltas)
    return pl.pallas_call(
        functools.partial(_filter_kernel, n_chunks=n_chunks),
        grid=(L // tl,),
        in_specs=[pl.BlockSpec((tl, LANES), lambda i: (i, 0))] + [full(a) for a in args[1:]],
        out_specs=pl.BlockSpec((n_chunks, tl, HY_WIDTH), lambda i: (0, i, 0)),
        out_shape=jax.ShapeDtypeStruct((n_chunks, L, HY_WIDTH), F32),
        compiler_params=_cparams("parallel"),
        name="hy_filter",
    )(*args)


def _dft_tables(L):
    P = DFT_P
    R = L // P
    Q = 2 * R
    N = 2 * L
    nk1 = Q // 2 + 1
    mr = -(-2 * nk1 // SUBLANES) * SUBLANES
    n2 = np.arange(P)[:, None, None]
    k1 = np.arange(nk1)[None, :, None]
    n1 = np.arange(R)[None, None, :]
    ph = 2.0 * np.pi * ((k1 * (P * n1 + n2)) % N) / N
    f1 = np.zeros((P, mr, R))
    f1[:, 0:2 * nk1:2, :] = np.cos(ph)
    f1[:, 1:2 * nk1:2, :] = -np.sin(ph)
    ck = np.where((np.arange(nk1) == 0) | (np.arange(nk1) == Q // 2), 1.0, 2.0)[None, :, None] / N
    g = np.zeros((P, R, mr))
    g[:, :, 0:2 * nk1:2] = np.transpose(ck * np.cos(ph), (0, 2, 1))
    g[:, :, 1:2 * nk1:2] = np.transpose(-ck * np.sin(ph), (0, 2, 1))
    a = np.arange(P)
    th = 2.0 * np.pi * ((a[:, None] * a[None, :]) % P) / P
    tr, ti = np.cos(th), -np.sin(th)
    d_fwd = np.block([[tr, -ti], [ti, tr]])
    d_inv = np.block([[tr, ti], [-ti, tr]])
    as_bf = lambda m: jnp.asarray(m, dtype=F32).astype(BF16)
    return dict(f1=as_bf(f1), g=as_bf(g), d_fwd=as_bf(d_fwd), d_inv=as_bf(d_inv), nk1=nk1, mr=mr, R=R)


def _swap_major_sublane(x):
    return jnp.swapaxes(x, 0, 1)


def _fwd_dft(u3s, a3, f1_ref, dfwd_ref, nk1, post):
    R, P, _ = u3s[0].shape
    W = a3.shape[2]

    def step1(g, _):
        n0 = pl.multiple_of(g * N2_GROUP, N2_GROUP)
        x = [u3[:, pl.ds(n0, N2_GROUP), :] for u3 in u3s]
        xt = _swap_major_sublane(x[0] if len(x) == 1 else jnp.concatenate(x, axis=-1))
        a = jnp.stack([jnp.dot(f1_ref[n0 + r], xt[r].astype(BF16), preferred_element_type=F32)
                       for r in range(N2_GROUP)])
        a3[:, pl.ds(n0, N2_GROUP), :] = _swap_major_sublane(a)
        return 0

    lax.fori_loop(0, P // N2_GROUP, step1, 0, unroll=OUTER_UNROLL)

    group = max(MXU_WIDTH // W, 1)

    def inner(k1, n):
        a = a3[pl.ds(2 * k1, 2 * n)].astype(BF16)
        a = [a[2 * i:2 * i + 2].reshape(2 * P, W) for i in range(n)]
        post(k1, n, jnp.dot(dfwd_ref[...], a[0] if n == 1 else jnp.concatenate(a, axis=1),
                            preferred_element_type=F32))

    n_groups = nk1 // group

    def step3(gi, _):
        inner(gi * group, group)
        return 0

    lax.fori_loop(0, n_groups, step3, 0, unroll=_inner_unroll(n_groups))
    for k1 in range(n_groups * group, nk1):
        inner(k1, 1)


def _filtfft_kernel(hf_ref, hb_ref, f1_ref, dfwd_ref, o_ref, a_scr, *, nk1):
    P = DFT_P

    def combine(k1, n, x):
        assert n == 1
        xf, xb = x[:, 0:LANES], x[:, LANES:2 * LANES]
        o_ref[0, k1] = jnp.concatenate([xf[0:P] + xb[0:P], xf[P:2 * P] - xb[P:2 * P]], axis=0).astype(o_ref.dtype)

    _fwd_dft([hf_ref.at[0], hb_ref.at[0]], a_scr, f1_ref, dfwd_ref, nk1, combine)


def _filter_spectra(filt, tabs):
    n_f, L, C = filt.shape
    P = DFT_P
    nk1, mr, R = tabs["nk1"], tabs["mr"], tabs["R"]
    filt4 = filt.reshape(n_f, R, P, C)
    kern = functools.partial(_filtfft_kernel, nk1=nk1)
    return pl.pallas_call(
        kern,
        grid=(n_f // 2, C // LANES),
        in_specs=[
            pl.BlockSpec((1, R, P, LANES), lambda o, c: (2 * o, 0, 0, c)),
            pl.BlockSpec((1, R, P, LANES), lambda o, c: (2 * o + 1, 0, 0, c)),
            _resident((P, mr, R), lambda o, c: (0, 0, 0)),
            _resident((2 * P, 2 * P), lambda o, c: (0, 0)),
        ],
        out_specs=pl.BlockSpec((1, nk1, 2 * P, LANES), lambda o, c: (o, 0, 0, c)),
        out_shape=jax.ShapeDtypeStruct((n_f // 2, nk1, 2 * P, C), BF16),
        scratch_shapes=[pltpu.VMEM((mr, P, 2 * LANES), F32)],
        compiler_params=_cparams("parallel", "parallel"),
        name="hy_filtfft",
    )(filt4, filt4, tabs["f1"], tabs["d_fwd"])


def _short_conv_chunk(u_ref, r0, rows, L, w_ref, b_ref):
    halo = 16
    c = u_ref[pl.ds(r0, rows), :].astype(F32)
    p0 = pl.multiple_of(jnp.maximum(r0 - halo, 0), halo)
    n0 = pl.multiple_of(jnp.minimum(r0 + rows, L - halo), halo)
    prev = u_ref[pl.ds(p0, halo), :].astype(F32)[halo - 1:halo]
    nxt = u_ref[pl.ds(n0, halo), :].astype(F32)[0:1]
    prev = jnp.where(r0 > 0, prev, 0.0)
    nxt = jnp.where(r0 + rows < L, nxt, 0.0)
    row = lax.broadcasted_iota(jnp.int32, (rows, 1), 0)
    up = jnp.where(row == 0, prev, pltpu.roll(c, 1, axis=0))
    dn = jnp.where(row == rows - 1, nxt, pltpu.roll(c, rows - 1, axis=0))
    return up * w_ref[0:1, :] + c * w_ref[1:2, :] + dn * w_ref[2:3, :] + b_ref[...]


def _hyconv_kernel(u_ref, gt_ref, kf_ref, f1_ref, g_ref, dfwd_ref, dinv_ref, wu_ref, bu_ref,
                   wg_ref, bg_ref, bias_ref, o_ref, u_scr, a_scr, y_scr, *, nk1, conv_u, chunk):
    R, P, _ = u_scr.shape
    L = R * P
    n_chunks = L // chunk
    cr = chunk // P

    def load_u(i, _):
        r0 = pl.multiple_of(i * chunk, chunk)
        if conv_u:
            u = _short_conv_chunk(u_ref, r0, chunk, L, wu_ref, bu_ref)
        else:
            u = u_ref[pl.ds(r0, chunk), :].astype(F32)
        u_scr[pl.ds(i * cr, cr)] = u.reshape(cr, P, LANES)
        return 0

    lax.fori_loop(0, n_chunks, load_u, 0)

    def spectral(k1, n, x):
        ys = []
        for i in range(n):
            kf = kf_ref[0, k1 + i].astype(F32)
            xs = x[:, i * LANES:(i + 1) * LANES]
            xr, xi = xs[0:P], xs[P:2 * P]
            kr, ki = kf[0:P], kf[P:2 * P]
            ys.append(jnp.concatenate([xr * kr - xi * ki, xr * ki + xi * kr], axis=0).astype(BF16))
        b = jnp.dot(dinv_ref[...], ys[0] if n == 1 else jnp.concatenate(ys, axis=1),
                    preferred_element_type=F32)
        for i in range(n):
            a_scr[pl.ds(2 * (k1 + i), 2)] = b[:, i * LANES:(i + 1) * LANES].reshape(2, P, LANES)

    _fwd_dft([u_scr], a_scr, f1_ref, dfwd_ref, nk1, spectral)

    def inv_outer(g, _):
        n0 = pl.multiple_of(g * N2_GROUP, N2_GROUP)
        bt = _swap_major_sublane(a_scr[:, pl.ds(n0, N2_GROUP), :])
        y = jnp.stack([jnp.dot(g_ref[n0 + r], bt[r].astype(BF16), preferred_element_type=F32)
                       for r in range(N2_GROUP)])
        y_scr[:, pl.ds(n0, N2_GROUP), :] = _swap_major_sublane(y)
        return 0

    lax.fori_loop(0, P // N2_GROUP, inv_outer, 0, unroll=OUTER_UNROLL)

    def epilogue(i, _):
        r0 = pl.multiple_of(i * chunk, chunk)
        gate = _short_conv_chunk(gt_ref, r0, chunk, L, wg_ref, bg_ref)
        rows = pl.ds(i * cr, cr)
        y = y_scr[rows].reshape(chunk, LANES) + u_scr[rows].reshape(chunk, LANES) * bias_ref[...]
        o_ref[pl.ds(r0, chunk), :] = (gate * y).astype(o_ref.dtype)
        return 0

    lax.fori_loop(0, n_chunks, epilogue, 0)


def _hyena_conv(u_arr, u_col0, gate_arr, gate_col0, kf, order, conv_w, conv_b, bias, tabs, B, L, conv_u):
    P = DFT_P
    nk1, mr, R = tabs["nk1"], tabs["mr"], tabs["R"]
    C = HY_WIDTH
    n_ct = C // LANES
    chunk = min(512, L)
    kern = functools.partial(_hyconv_kernel, nk1=nk1, conv_u=conv_u, chunk=chunk)
    ucb, gcb = u_col0 // LANES, gate_col0 // LANES
    return pl.pallas_call(
        kern,
        grid=(n_ct, B),
        in_specs=[
            pl.BlockSpec((L, LANES), lambda c, b: (b, ucb + c)),
            pl.BlockSpec((L, LANES), lambda c, b: (b, gcb + c)),
            pl.BlockSpec((1, nk1, 2 * P, LANES), lambda c, b: (order, 0, 0, c)),
            _resident((P, mr, R), lambda c, b: (0, 0, 0)),
            _resident((P, R, mr), lambda c, b: (0, 0, 0)),
            _resident((2 * P, 2 * P), lambda c, b: (0, 0)),
            _resident((2 * P, 2 * P), lambda c, b: (0, 0)),
            pl.BlockSpec((3, LANES), lambda c, b: (0, ucb + c if conv_u else c)),
            pl.BlockSpec((1, LANES), lambda c, b: (0, ucb + c if conv_u else c)),
            pl.BlockSpec((3, LANES), lambda c, b: (0, gcb + c)),
            pl.BlockSpec((1, LANES), lambda c, b: (0, gcb + c)),
            pl.BlockSpec((1, LANES), lambda c, b: (0, c)),
        ],
        out_specs=pl.BlockSpec((L, LANES), lambda c, b: (b, c)),
        out_shape=jax.ShapeDtypeStruct((B * L, C), BF16),
        scratch_shapes=[
            pltpu.VMEM((R, P, LANES), F32),
            pltpu.VMEM((mr, P, LANES), F32),
            pltpu.VMEM((R, P, LANES), F32),
        ],
        compiler_params=_cparams("parallel", "arbitrary"),
        name=f"hy_conv{order}",
    )(u_arr, gate_arr, kf, tabs["f1"], tabs["g"], tabs["d_fwd"], tabs["d_inv"],
      conv_w, conv_b, conv_w, conv_b, bias[order].reshape(1, C))


def _attn_kernel(q_ref, k_ref, v_ref, lamv_ref, sg_ref, o_ref, vt_scr, m_scr, acc_scr,
                 *score_bufs, lam_init):
    n_kv, vrows, tk = vt_scr.shape
    dv = v_ref.shape[1]
    d = ATT_HEAD_DIM

    @pl.when(pl.program_id(2) == 0)
    def _():
        for c in range(n_kv):
            vt_scr[c, 0:dv, :] = v_ref[c * tk:(c + 1) * tk, :].astype(F32).T.astype(BF16)
            vt_scr[c, dv:vrows, :] = jnp.ones((vrows - dv, tk), BF16)

    qt = q_ref[...].astype(F32).T
    row = lax.broadcasted_iota(jnp.int32, (2 * d, 1), 0)
    qts = (jnp.where(row < d, qt, 0.0).astype(BF16), jnp.where(row >= d, qt, 0.0).astype(BF16))
    m_scr[...] = jnp.full(m_scr.shape, -jnp.inf, F32)
    acc_scr[...] = jnp.zeros(acc_scr.shape, F32)

    tq = q_ref.shape[0]

    def scores(j, s_ref, cm_ref, q0):
        kc = k_ref[pl.ds(pl.multiple_of(j * tk, tk), tk), :]
        qs = slice(q0, q0 + ATTN_QUERY_TILE)
        for comp in range(2):
            s = jnp.dot(kc, qts[comp][:, qs], preferred_element_type=F32)
            s_ref[comp, :, qs] = s
            cm_ref[comp, :, qs] = jnp.max(s, axis=0, keepdims=True)

    def accumulate(j, s_ref, cm_ref, q0):
        vt = vt_scr[j]
        qs = slice(q0, q0 + ATTN_QUERY_TILE)
        for comp in range(2):
            m_old = m_scr[comp, :, qs]
            m_new = jnp.maximum(m_old, cm_ref[comp, :, qs])
            p = jnp.exp2(s_ref[comp, :, qs] - m_new).astype(BF16)
            acc_scr[comp, :, qs] = (jnp.exp2(m_old - m_new) * acc_scr[comp, :, qs]
                                    + jnp.dot(vt, p, preferred_element_type=F32))
            m_scr[comp, :, qs] = m_new

    def step(score_job, acc_job):
        for q0 in range(0, tq, ATTN_QUERY_TILE):
            if score_job is not None:
                scores(*score_job, q0)
            if acc_job is not None:
                accumulate(*acc_job, q0)

    nbuf = len(score_bufs) // 2
    bufs = [(score_bufs[i], score_bufs[nbuf + i]) for i in range(nbuf)]
    depth = nbuf - 1
    for c in range(min(depth, n_kv)):
        step((c, *bufs[c % nbuf]), None)
    trips = max((n_kv - depth) // nbuf, 0)

    def body(jj, _):
        j = jj * nbuf
        for t in range(nbuf):
            step((j + t + depth, *bufs[(t + depth) % nbuf]), (j + t, *bufs[t]))
        return 0

    if trips > 0:
        lax.fori_loop(0, trips, body, 0, unroll=ATTN_TRIP_UNROLL)
    for c in range(trips * nbuf, n_kv):
        later = (c + depth, *bufs[(c + depth) % nbuf]) if c + depth < n_kv else None
        step(later, (c, *bufs[c % nbuf]))

    lv = lamv_ref[...]
    lam = (jnp.exp(jnp.sum(lv[0:1] * lv[1:2], axis=-1, keepdims=True))
           - jnp.exp(jnp.sum(lv[2:3] * lv[3:4], axis=-1, keepdims=True)) + lam_init)
    a1, a2 = acc_scr[0], acc_scr[1]
    o = a1[0:dv] / a1[dv:dv + 1] - lam * (a2[0:dv] / a2[dv:dv + 1])
    ms = jnp.mean(o * o, axis=0, keepdims=True)
    o = o * lax.rsqrt(ms + EPS) * sg_ref[...] * (1.0 - lam_init)
    o_ref[...] = o.T.astype(o_ref.dtype)


def _diff_attention(proj, q_col0, k_col0, v_col0, lam_vecs, subln_g, lam_init, B, L):
    tq = min(2048, L)
    tk = min(512, L)
    hw = ATT_V_DIM
    vrows = hw + 16
    qcb, kcb, vcb = q_col0 // hw, k_col0 // hw, v_col0 // hw
    nq = L // tq
    kern = functools.partial(_attn_kernel, lam_init=lam_init)
    return pl.pallas_call(
        kern,
        grid=(B, ATT_HEADS, nq),
        in_specs=[
            pl.BlockSpec((tq, hw), lambda b, h, i: (b * nq + i, qcb + h)),
            pl.BlockSpec((L, hw), lambda b, h, i: (b, kcb + h)),
            pl.BlockSpec((L, hw), lambda b, h, i: (b, vcb + h)),
            pl.BlockSpec((4, ATT_HEAD_DIM), lambda b, h, i: (0, 0)),
            pl.BlockSpec((hw, 1), lambda b, h, i: (0, 0)),
        ],
        out_specs=pl.BlockSpec((tq, hw), lambda b, h, i: (b * nq + i, h)),
        out_shape=jax.ShapeDtypeStruct((B * L, ATT_HEADS * hw), BF16),
        scratch_shapes=[
            pltpu.VMEM((L // tk, vrows, tk), BF16),
            pltpu.VMEM((2, 1, tq), F32),
            pltpu.VMEM((2, vrows, tq), F32),
        ] + [pltpu.VMEM((2, tk, tq), F32)] * ATTN_SCORE_BUFS + [pltpu.VMEM((2, 1, tq), F32)] * ATTN_SCORE_BUFS,
        compiler_params=_cparams("parallel", "parallel", "arbitrary"),
        name="diff_attn",
    )(proj, proj, proj, lam_vecs, subln_g.reshape(hw, 1))


def _merge_kernel(yh_ref, ya_ref, gh_ref, ga_ref, wph_ref, wpa_ref, wo_ref, x_ref, mod_ref, o_ref):
    mh = jnp.dot(yh_ref[...], wph_ref[...], preferred_element_type=F32)
    ma = jnp.dot(ya_ref[...], wpa_ref[...], preferred_element_type=F32)
    merged = (gh_ref[...].astype(F32) * mh + ga_ref[...].astype(F32) * ma).astype(BF16)
    r = jnp.dot(merged, wo_ref[...], preferred_element_type=F32)
    o_ref[...] = x_ref[...] + mod_ref[0, 2:3, :] * r


def _merge_out(y_hy, y_att, proj, gate_col0, wph, wpa, wo, x2, mod, seq_len):
    T, D = x2.shape
    tm = 512
    bps = seq_len // tm
    gcb = gate_col0 // D
    cw = y_hy.shape[1]
    return pl.pallas_call(
        _merge_kernel,
        grid=(T // tm,),
        in_specs=[
            pl.BlockSpec((tm, cw), lambda i: (i, 0)),
            pl.BlockSpec((tm, cw), lambda i: (i, 0)),
            pl.BlockSpec((tm, D), lambda i: (i, gcb)),
            pl.BlockSpec((tm, D), lambda i: (i, gcb + 1)),
            _resident((cw, D), lambda i: (0, 0)),
            _resident((cw, D), lambda i: (0, 0)),
            _resident((D, D), lambda i: (0, 0)),
            pl.BlockSpec((tm, D), lambda i: (i, 0)),
            pl.BlockSpec((1, N_MOD, D), lambda i: (i // bps, 0, 0)),
        ],
        out_specs=pl.BlockSpec((tm, D), lambda i: (i, 0)),
        out_shape=jax.ShapeDtypeStruct((T, D), F32),
        compiler_params=_cparams("parallel"),
        name="merge_out",
    )(y_hy, y_att, proj, proj, wph, wpa, wo, x2, mod)


def _ffn_kernel(x_ref, mod_ref, g_ref, wg_ref, wu_ref, wd_ref, o_ref, h_scr, acc_scr):
    j = pl.program_id(1)
    tm = h_scr.shape[0]
    th = wd_ref.shape[0]

    def swiglu_down(h):
        down = None
        for n0 in range(0, th, FFN_HIDDEN_TILE):
            ns = slice(n0, n0 + FFN_HIDDEN_TILE)
            a = jnp.dot(h, wg_ref[:, ns], preferred_element_type=F32)
            u = jnp.dot(h, wu_ref[:, ns], preferred_element_type=F32)
            f = (a * jax.nn.sigmoid(a) * u).astype(BF16)
            part = jnp.dot(f, wd_ref[ns, :], preferred_element_type=F32)
            down = part if down is None else down + part
        return down

    @pl.when(j == 0)
    def _():
        for r in range(tm // FFN_ROW_CHUNK):
            rows = pl.ds(r * FFN_ROW_CHUNK, FFN_ROW_CHUNK)
            h = _modulated_norm(x_ref[rows, :], g_ref[...], mod_ref[0, 3:4, :], mod_ref[0, 4:5, :]).astype(BF16)
            h_scr[rows, :] = h
            acc_scr[rows, :] = swiglu_down(h)

    @pl.when(j > 0)
    def _():
        acc_scr[...] += swiglu_down(h_scr[...])

    @pl.when(j == pl.num_programs(1) - 1)
    def _():
        o_ref[...] = x_ref[...] + mod_ref[0, 5:6, :] * acc_scr[...]


def _ffn(x2, mod, norm_g, wg, wu, wd, seq_len):
    T, D = x2.shape
    H = wg.shape[1]
    tm, th = 512, 512
    bps = seq_len // tm
    return pl.pallas_call(
        _ffn_kernel,
        grid=(T // tm, H // th),
        in_specs=[
            pl.BlockSpec((tm, D), lambda i, j: (i, 0)),
            pl.BlockSpec((1, N_MOD, D), lambda i, j: (i // bps, 0, 0)),
            pl.BlockSpec((1, D), lambda i, j: (0, 0)),
            pl.BlockSpec((D, th), lambda i, j: (0, j)),
            pl.BlockSpec((D, th), lambda i, j: (0, j)),
            pl.BlockSpec((th, D), lambda i, j: (j, 0)),
        ],
        out_specs=pl.BlockSpec((tm, D), lambda i, j: (i, 0)),
        out_shape=jax.ShapeDtypeStruct((T, D), F32),
        scratch_shapes=[pltpu.VMEM((tm, D), BF16), pltpu.VMEM((tm, D), F32)],
        compiler_params=_cparams("parallel", "arbitrary"),
        name="ffn",
    )(x2, mod, norm_g.reshape(1, D), wg, wu, wd)


def kernel(x, c, positions, w_ada, b_ada, norm1_g, w_in, hy_conv_w, hy_conv_b, hy_filt_w1, hy_filt_b1, hy_filt_w2, hy_filt_b2, hy_filt_w3, hy_filt_b3, hy_filt_freq, hy_filt_w_out, hy_bias, q_norm_g, k_norm_g, lam_q1, lam_k1, lam_q2, lam_k2, subln_g, w_proj_hy, w_proj_att, w_out, norm2_g, w_gate, w_up, w_down):
    B, L, D = x.shape
    T = B * L
    depth = w_ada.shape[0]
    s_hy = (HY_ORDER + 1) * HY_WIDTH
    qk_w = ATT_HEADS * 2 * ATT_HEAD_DIM
    q0, k0 = s_hy, s_hy + qk_w
    v0 = k0 + qk_w
    g0 = v0 + ATT_HEADS * ATT_V_DIM
    tabs = _dft_tables(L)
    pos_col = positions.astype(F32).reshape(T, 1)
    x2 = x.reshape(T, D)
    for l in range(depth):
        lam_init = 0.8 - 0.6 * math.exp(-0.3 * l)
        mod = _ada_mod(c, w_ada[l], b_ada[l])
        proj = _in_proj(x2, mod, norm1_g[l], w_in[l].astype(BF16), pos_col,
                        q_norm_g[l], k_norm_g[l], L)

        filt = _hyena_filters(L, hy_filt_w1[l], hy_filt_b1[l], hy_filt_w2[l], hy_filt_b2[l],
                              hy_filt_w3[l], hy_filt_b3[l], hy_filt_freq[l], hy_filt_w_out[l])
        kf = _filter_spectra(filt, tabs)
        cw, cb = hy_conv_w[l], hy_conv_b[l].reshape(1, s_hy)
        z = _hyena_conv(proj, 0, proj, HY_WIDTH, kf, 0, cw, cb, hy_bias[l], tabs, B, L, True)
        y_hy = _hyena_conv(z, 0, proj, 2 * HY_WIDTH, kf, 1, cw, cb, hy_bias[l], tabs, B, L, False)

        lam_vecs = jnp.stack([lam_q1[l], lam_k1[l], lam_q2[l], lam_k2[l]]).astype(F32)
        y_att = _diff_attention(proj, q0, k0, v0, lam_vecs, subln_g[l], lam_init, B, L)

        x2 = _merge_out(y_hy, y_att, proj, g0, w_proj_hy[l].astype(BF16), w_proj_att[l].astype(BF16),
                        w_out[l].astype(BF16), x2, mod, L)
        x2 = _ffn(x2, mod, norm2_g[l], w_gate[l].astype(BF16), w_up[l].astype(BF16),
                  w_down[l].astype(BF16), L)
    return x2.reshape(B, L, D)
```

```python
import functools
import math

import numpy as np
import jax
import jax.numpy as jnp
from jax import lax
from jax.experimental import pallas as pl
from jax.experimental.pallas import tpu as pltpu

F32 = jnp.float32
BF16 = jnp.bfloat16

ATT_HEADS = 8
ATT_HEAD_DIM = 64
ATT_V_DIM = 2 * ATT_HEAD_DIM
ROPE_DIM = ATT_HEAD_DIM // 4
ROPE_THETA = 500000.0
HY_WIDTH = 1024
HY_ORDER = 2
HY_EMB_BANDS = 16
HY_DECAY_TARGET = 1e-2
HY_MIN_DECAY = math.log(HY_DECAY_TARGET) / 0.3
HY_MAX_DECAY = math.log(HY_DECAY_TARGET) / 1.5
N_MOD = 6
EPS = 1e-6
LOG2_E = math.log2(math.e)

LANES = 128
SUBLANES = 8
MXU_WIDTH = 256
VMEM_LIMIT_BYTES = 56 * 1024 * 1024

ATTN_SCORE_BUFS = 2
ATTN_QUERY_TILE = MXU_WIDTH
ATTN_TRIP_UNROLL = 2
FFN_HIDDEN_TILE = MXU_WIDTH
FFN_ROW_CHUNK = 256
INPROJ_ROW_CHUNK = 256
DFT_P = LANES
N2_GROUP = SUBLANES
OUTER_UNROLL = 4


def _inner_unroll(nk1):
    return next((u for u in (16, 13, 8, 5, 4, 3, 2) if nk1 % u == 0), 1)


def _cparams(*sem):
    return pltpu.CompilerParams(dimension_semantics=sem, vmem_limit_bytes=VMEM_LIMIT_BYTES)


def _resident(block_shape, index_map):
    return pl.BlockSpec(block_shape, index_map, pipeline_mode=pl.Buffered(1))


def _ada_kernel(ct_ref, w_ref, b_ref, o_ref):
    w = w_ref[...]
    for b in range(ct_ref.shape[0]):
        cb = ct_ref[b]
        s = cb * jax.nn.sigmoid(cb)
        o_ref[b] = jnp.sum(w * s, axis=0, keepdims=True) + b_ref[...]


def _ada_mod(c, w_ada, b_ada):
    B, D = c.shape
    n = w_ada.shape[1]
    tn = 512
    out = pl.pallas_call(
        _ada_kernel,
        grid=(n // tn,),
        in_specs=[
            pl.BlockSpec((B, D, 1), lambda j: (0, 0, 0)),
            pl.BlockSpec((D, tn), lambda j: (0, j)),
            pl.BlockSpec((1, tn), lambda j: (0, j)),
        ],
        out_specs=pl.BlockSpec((B, 1, tn), lambda j: (0, 0, j)),
        out_shape=jax.ShapeDtypeStruct((B, 1, n), F32),
        compiler_params=_cparams("arbitrary"),
        name="ada_mod",
    )(c.reshape(B, D, 1), w_ada, b_ada.reshape(1, n))
    return out.reshape(B, N_MOD, D)


def _modulated_norm(x, g, shift, scale):
    ms = jnp.mean(x * x, axis=-1, keepdims=True)
    return (x * lax.rsqrt(ms + EPS) * g) * (1.0 + scale) + shift


def _inproj_kernel(x_ref, mod_ref, g_ref, w_ref, pos_ref, qg_ref, kg_ref, inv_ref, bd_ref,
                   o_ref, h_scr, rope_scr, *, jq, jk, jgate):
    j = pl.program_id(1)

    tm, tn = o_ref.shape

    @pl.when(j == 0)
    def _():
        for r in range(tm // INPROJ_ROW_CHUNK):
            rows = pl.ds(r * INPROJ_ROW_CHUNK, INPROJ_ROW_CHUNK)
            h = _modulated_norm(x_ref[rows, :], g_ref[...], mod_ref[0, 0:1, :], mod_ref[0, 1:2, :]).astype(BF16)
            h_scr[rows, :] = h
            o_ref[rows, :] = jnp.dot(h, w_ref[...], preferred_element_type=F32).astype(o_ref.dtype)
        ang = pos_ref[...] * inv_ref[...]
        cos, sin = jnp.cos(ang), jnp.sin(ang)
        lane = lax.broadcasted_iota(jnp.int32, (1, LANES), 1) % ATT_HEAD_DIM
        half = ROPE_DIM // 2
        rope_scr[0] = jnp.where(lane < ROPE_DIM, cos, 1.0)
        rope_scr[1] = jnp.where(lane < half, -sin, jnp.where(lane < ROPE_DIM, sin, 0.0))

    def project(epilogue):
        for r in range(tm // INPROJ_ROW_CHUNK):
            rows = pl.ds(r * INPROJ_ROW_CHUNK, INPROJ_ROW_CHUNK)
            acc = jnp.dot(h_scr[rows, :], w_ref[...], preferred_element_type=F32)
            o_ref[rows, :] = epilogue(acc, rows).astype(o_ref.dtype)

    def qk_norm_rope(gain_ref, out_scale):
        def epilogue(acc, rows):
            sq = (acc * acc).astype(BF16)
            bw = bd_ref.shape[0]
            ms = jnp.concatenate(
                [jnp.dot(sq[:, g * bw:(g + 1) * bw], bd_ref[...], preferred_element_type=F32)
                 for g in range(tn // bw)], axis=1)
            y = acc * lax.rsqrt(ms + EPS) * gain_ref[...]
            reps = tn // LANES
            c_t = jnp.tile(rope_scr[0, rows, :], (1, reps))
            s_t = jnp.tile(rope_scr[1, rows, :], (1, reps))
            half = ROPE_DIM // 2
            lane = lax.broadcasted_iota(jnp.int32, (1, tn), 1) % ATT_HEAD_DIM
            y_pair = jnp.where(lane < half, pltpu.roll(y, tn - half, axis=1), pltpu.roll(y, half, axis=1))
            return (y * c_t + y_pair * s_t) * out_scale
        return epilogue

    @pl.when(j == jq)
    def _():
        project(qk_norm_rope(qg_ref, ATT_HEAD_DIM ** -0.5 * LOG2_E))

    @pl.when(j == jk)
    def _():
        project(qk_norm_rope(kg_ref, 1.0))

    @pl.when(j >= jgate)
    def _():
        project(lambda acc, rows: 0.5 * jnp.tanh(0.5 * acc) + 0.5)

    @pl.when((j != 0) & (j != jq) & (j != jk) & (j < jgate))
    def _():
        project(lambda acc, rows: acc)


def _in_proj(x2, mod, norm_g, w_in_bf, pos_col, q_g, k_g, seq_len):
    T, D = x2.shape
    n = w_in_bf.shape[1]
    tm, tn = min(1024, seq_len), 1024
    blocks_per_seq = seq_len // tm
    s_hy = (HY_ORDER + 1) * HY_WIDTH
    qk_w = ATT_HEADS * 2 * ATT_HEAD_DIM
    assert tn == qk_w and s_hy % tn == 0 and s_hy >= tn
    jq = s_hy // tn
    jk = jq + 1
    jgate = jk + 2
    half = ROPE_DIM // 2
    inv = ROPE_THETA ** (-jnp.arange(half, dtype=F32) * (2.0 / ROPE_DIM))
    lane = np.arange(LANES) % ATT_HEAD_DIM
    inv_lane = jnp.where(lane < ROPE_DIM, inv[lane % half], 0.0).reshape(1, LANES).astype(F32)
    grp = np.arange(MXU_WIDTH) // ATT_HEAD_DIM
    bd = jnp.asarray((grp[:, None] == grp[None, :]).astype(np.float32) / ATT_HEAD_DIM, BF16)
    tile_g = lambda g: jnp.tile(g.reshape(1, ATT_HEAD_DIM), (1, tn // ATT_HEAD_DIM))
    kern = functools.partial(_inproj_kernel, jq=jq, jk=jk, jgate=jgate)
    return pl.pallas_call(
        kern,
        grid=(T // tm, n // tn),
        in_specs=[
            pl.BlockSpec((tm, D), lambda i, j: (i, 0)),
            pl.BlockSpec((1, N_MOD, D), lambda i, j: (i // blocks_per_seq, 0, 0)),
            pl.BlockSpec((1, D), lambda i, j: (0, 0)),
            pl.BlockSpec((D, tn), lambda i, j: (0, j)),
            pl.BlockSpec((tm, 1), lambda i, j: (i, 0)),
            pl.BlockSpec((1, tn), lambda i, j: (0, 0)),
            pl.BlockSpec((1, tn), lambda i, j: (0, 0)),
            pl.BlockSpec((1, LANES), lambda i, j: (0, 0)),
            pl.BlockSpec((MXU_WIDTH, MXU_WIDTH), lambda i, j: (0, 0)),
        ],
        out_specs=pl.BlockSpec((tm, tn), lambda i, j: (i, j)),
        out_shape=jax.ShapeDtypeStruct((T, n), BF16),
        scratch_shapes=[pltpu.VMEM((tm, D), BF16), pltpu.VMEM((2, tm, LANES), F32)],
        compiler_params=_cparams("parallel", "arbitrary"),
        name="in_proj",
    )(x2, mod, norm_g.reshape(1, D), w_in_bf, pos_col, tile_g(q_g), tile_g(k_g), inv_lane, bd)


def _filter_kernel(z_ref, w1_ref, b1_ref, w2_ref, b2_ref, w3_ref, b3_ref, fr_ref, wo_ref, dl_ref,
                   o_ref, *, n_chunks):
    tl = z_ref.shape[0]
    z = z_ref[...]
    h = jnp.sin(fr_ref[0:1, :] * (jnp.dot(z, w1_ref[...], preferred_element_type=F32) + b1_ref[...]))
    h = jnp.sin(fr_ref[1:2, :] * (jnp.dot(h, w2_ref[...], preferred_element_type=F32) + b2_ref[...]))
    h = jnp.sin(fr_ref[2:3, :] * (jnp.dot(h, w3_ref[...], preferred_element_type=F32) + b3_ref[...]))
    t = z[:, 0:1]
    decay = jnp.exp(-t * dl_ref[...])
    row = pl.program_id(0) * tl + lax.broadcasted_iota(jnp.int32, (tl, 1), 0)
    not_first = (row != 0).astype(F32)
    width = dl_ref.shape[1]
    for ci in range(n_chunks):
        f = jnp.dot(h, wo_ref[:, ci * width:(ci + 1) * width], preferred_element_type=F32) * decay
        if ci % 2 == 1:
            f = f * not_first
        o_ref[ci] = f


def _hyena_filters(L, w1, b1, w2, b2, w3, b3, freq, w_out):
    emb = w1.shape[0]
    hid = w1.shape[1]
    t = jnp.linspace(0.0, 1.0, L, dtype=F32)[:, None]
    w = (2.0 * math.pi / L) * jnp.arange(L, dtype=F32)[:, None]
    bands = jnp.linspace(1e-4, HY_EMB_BANDS - 1, HY_EMB_BANDS, dtype=F32)
    z = jnp.concatenate([t, jnp.cos(bands * w), -jnp.sin(bands * w)], axis=-1)
    z = jnp.pad(z, ((0, 0), (0, LANES - emb)))
    w1p = jnp.pad(w1, ((0, LANES - emb), (0, 0)))
    deltas = jnp.abs(jnp.linspace(HY_MIN_DECAY, HY_MAX_DECAY, HY_WIDTH, dtype=F32)).reshape(1, HY_WIDTH)
    n_chunks = w_out.shape[1] // HY_WIDTH
    tl = min(512, L)
    full = lambda a: pl.BlockSpec(a.shape, lambda i: (0,) * a.ndim)
    b1, b2, b3 = (b.reshape(1, hid) for b in (b1, b2, b3))
    args = (z, w1p, b1, w2, b2, w3, b3, freq, w_out, deltas)
    return pl.pallas_call(
        functools.partial(_filter_kernel, n_chunks=n_chunks),
        grid=(L // tl,),
        in_specs=[pl.BlockSpec((tl, LANES), lambda i: (i, 0))] + [full(a) for a in args[1:]],
        out_specs=pl.BlockSpec((n_chunks, tl, HY_WIDTH), lambda i: (0, i, 0)),
        out_shape=jax.ShapeDtypeStruct((n_chunks, L, HY_WIDTH), F32),
        compiler_params=_cparams("parallel"),
        name="hy_filter",
    )(*args)


def _dft_tables(L):
    P = DFT_P
    R = L // P
    Q = 2 * R
    N = 2 * L
    nk1 = Q // 2 + 1
    mr = -(-2 * nk1 // SUBLANES) * SUBLANES
    n2 = np.arange(P)[:, None, None]
    k1 = np.arange(nk1)[None, :, None]
    n1 = np.arange(R)[None, None, :]
    ph = 2.0 * np.pi * ((k1 * (P * n1 + n2)) % N) / N
    f1 = np.zeros((P, mr, R))
    f1[:, 0:2 * nk1:2, :] = np.cos(ph)
    f1[:, 1:2 * nk1:2, :] = -np.sin(ph)
    ck = np.where((np.arange(nk1) == 0) | (np.arange(nk1) == Q // 2), 1.0, 2.0)[None, :, None] / N
    g = np.zeros((P, R, mr))
    g[:, :, 0:2 * nk1:2] = np.transpose(ck * np.cos(ph), (0, 2, 1))
    g[:, :, 1:2 * nk1:2] = np.transpose(-ck * np.sin(ph), (0, 2, 1))
    a = np.arange(P)
    th = 2.0 * np.pi * ((a[:, None] * a[None, :]) % P) / P
    tr, ti = np.cos(th), -np.sin(th)
    d_fwd = np.block([[tr, -ti], [ti, tr]])
    d_inv = np.block([[tr, ti], [-ti, tr]])
    as_bf = lambda m: jnp.asarray(m, dtype=F32).astype(BF16)
    return dict(f1=as_bf(f1), g=as_bf(g), d_fwd=as_bf(d_fwd), d_inv=as_bf(d_inv), nk1=nk1, mr=mr, R=R)


def _swap_major_sublane(x):
    return jnp.swapaxes(x, 0, 1)


def _fwd_dft(u3s, a3, f1_ref, dfwd_ref, nk1, post):
    R, P, _ = u3s[0].shape
    W = a3.shape[2]

    def step1(g, _):
        n0 = pl.multiple_of(g * N2_GROUP, N2_GROUP)
        x = [u3[:, pl.ds(n0, N2_GROUP), :] for u3 in u3s]
        xt = _swap_major_sublane(x[0] if len(x) == 1 else jnp.concatenate(x, axis=-1))
        a = jnp.stack([jnp.dot(f1_ref[n0 + r], xt[r].astype(BF16), preferred_element_type=F32)
                       for r in range(N2_GROUP)])
        a3[:, pl.ds(n0, N2_GROUP), :] = _swap_major_sublane(a)
        return 0

    lax.fori_loop(0, P // N2_GROUP, step1, 0, unroll=OUTER_UNROLL)

    group = max(MXU_WIDTH // W, 1)

    def inner(k1, n):
        a = a3[pl.ds(2 * k1, 2 * n)].astype(BF16)
        a = [a[2 * i:2 * i + 2].reshape(2 * P, W) for i in range(n)]
        post(k1, n, jnp.dot(dfwd_ref[...], a[0] if n == 1 else jnp.concatenate(a, axis=1),
                            preferred_element_type=F32))

    n_groups = nk1 // group

    def step3(gi, _):
        inner(gi * group, group)
        return 0

    lax.fori_loop(0, n_groups, step3, 0, unroll=_inner_unroll(n_groups))
    for k1 in range(n_groups * group, nk1):
        inner(k1, 1)


def _filtfft_kernel(hf_ref, hb_ref, f1_ref, dfwd_ref, o_ref, a_scr, *, nk1):
    P = DFT_P

    def combine(k1, n, x):
        assert n == 1
        xf, xb = x[:, 0:LANES], x[:, LANES:2 * LANES]
        o_ref[0, k1] = jnp.concatenate([xf[0:P] + xb[0:P], xf[P:2 * P] - xb[P:2 * P]], axis=0).astype(o_ref.dtype)

    _fwd_dft([hf_ref.at[0], hb_ref.at[0]], a_scr, f1_ref, dfwd_ref, nk1, combine)


def _filter_spectra(filt, tabs):
    n_f, L, C = filt.shape
    P = DFT_P
    nk1, mr, R = tabs["nk1"], tabs["mr"], tabs["R"]
    filt4 = filt.reshape(n_f, R, P, C)
    kern = functools.partial(_filtfft_kernel, nk1=nk1)
    return pl.pallas_call(
        kern,
        grid=(n_f // 2, C // LANES),
        in_specs=[
            pl.BlockSpec((1, R, P, LANES), lambda o, c: (2 * o, 0, 0, c)),
            pl.BlockSpec((1, R, P, LANES), lambda o, c: (2 * o + 1, 0, 0, c)),
            _resident((P, mr, R), lambda o, c: (0, 0, 0)),
            _resident((2 * P, 2 * P), lambda o, c: (0, 0)),
        ],
        out_specs=pl.BlockSpec((1, nk1, 2 * P, LANES), lambda o, c: (o, 0, 0, c)),
        out_shape=jax.ShapeDtypeStruct((n_f // 2, nk1, 2 * P, C), BF16),
        scratch_shapes=[pltpu.VMEM((mr, P, 2 * LANES), F32)],
        compiler_params=_cparams("parallel", "parallel"),
        name="hy_filtfft",
    )(filt4, filt4, tabs["f1"], tabs["d_fwd"])


def _short_conv_chunk(u_ref, r0, rows, L, w_ref, b_ref):
    halo = 16
    c = u_ref[pl.ds(r0, rows), :].astype(F32)
    p0 = pl.multiple_of(jnp.maximum(r0 - halo, 0), halo)
    n0 = pl.multiple_of(jnp.minimum(r0 + rows, L - halo), halo)
    prev = u_ref[pl.ds(p0, halo), :].astype(F32)[halo - 1:halo]
    nxt = u_ref[pl.ds(n0, halo), :].astype(F32)[0:1]
    prev = jnp.where(r0 > 0, prev, 0.0)
    nxt = jnp.where(r0 + rows < L, nxt, 0.0)
    row = lax.broadcasted_iota(jnp.int32, (rows, 1), 0)
    up = jnp.where(row == 0, prev, pltpu.roll(c, 1, axis=0))
    dn = jnp.where(row == rows - 1, nxt, pltpu.roll(c, rows - 1, axis=0))
    return up * w_ref[0:1, :] + c * w_ref[1:2, :] + dn * w_ref[2:3, :] + b_ref[...]


def _hyconv_kernel(u_ref, gt_ref, kf_ref, f1_ref, g_ref, dfwd_ref, dinv_ref, wu_ref, bu_ref,
                   wg_ref, bg_ref, bias_ref, o_ref, u_scr, a_scr, y_scr, *, nk1, conv_u, chunk):
    R, P, _ = u_scr.shape
    L = R * P
    n_chunks = L // chunk
    cr = chunk // P

    def load_u(i, _):
        r0 = pl.multiple_of(i * chunk, chunk)
        if conv_u:
            u = _short_conv_chunk(u_ref, r0, chunk, L, wu_ref, bu_ref)
        else:
            u = u_ref[pl.ds(r0, chunk), :].astype(F32)
        u_scr[pl.ds(i * cr, cr)] = u.reshape(cr, P, LANES)
        return 0

    lax.fori_loop(0, n_chunks, load_u, 0)

    def spectral(k1, n, x):
        ys = []
        for i in range(n):
            kf = kf_ref[0, k1 + i].astype(F32)
            xs = x[:, i * LANES:(i + 1) * LANES]
            xr, xi = xs[0:P], xs[P:2 * P]
            kr, ki = kf[0:P], kf[P:2 * P]
            ys.append(jnp.concatenate([xr * kr - xi * ki, xr * ki + xi * kr], axis=0).astype(BF16))
        b = jnp.dot(dinv_ref[...], ys[0] if n == 1 else jnp.concatenate(ys, axis=1),
                    preferred_element_type=F32)
        for i in range(n):
            a_scr[pl.ds(2 * (k1 + i), 2)] = b[:, i * LANES:(i + 1) * LANES].reshape(2, P, LANES)

    _fwd_dft([u_scr], a_scr, f1_ref, dfwd_ref, nk1, spectral)

    def inv_outer(g, _):
        n0 = pl.multiple_of(g * N2_GROUP, N2_GROUP)
        bt = _swap_major_sublane(a_scr[:, pl.ds(n0, N2_GROUP), :])
        y = jnp.stack([jnp.dot(g_ref[n0 + r], bt[r].astype(BF16), preferred_element_type=F32)
                       for r in range(N2_GROUP)])
        y_scr[:, pl.ds(n0, N2_GROUP), :] = _swap_major_sublane(y)
        return 0

    lax.fori_loop(0, P // N2_GROUP, inv_outer, 0, unroll=OUTER_UNROLL)

    def epilogue(i, _):
        r0 = pl.multiple_of(i * chunk, chunk)
        gate = _short_conv_chunk(gt_ref, r0, chunk, L, wg_ref, bg_ref)
        rows = pl.ds(i * cr, cr)
        y = y_scr[rows].reshape(chunk, LANES) + u_scr[rows].reshape(chunk, LANES) * bias_ref[...]
        o_ref[pl.ds(r0, chunk), :] = (gate * y).astype(o_ref.dtype)
        return 0

    lax.fori_loop(0, n_chunks, epilogue, 0)


def _hyena_conv(u_arr, u_col0, gate_arr, gate_col0, kf, order, conv_w, conv_b, bias, tabs, B, L, conv_u):
    P = DFT_P
    nk1, mr, R = tabs["nk1"], tabs["mr"], tabs["R"]
    C = HY_WIDTH
    n_ct = C // LANES
    chunk = min(512, L)
    kern = functools.partial(_hyconv_kernel, nk1=nk1, conv_u=conv_u, chunk=chunk)
    ucb, gcb = u_col0 // LANES, gate_col0 // LANES
    return pl.pallas_call(
        kern,
        grid=(n_ct, B),
        in_specs=[
            pl.BlockSpec((L, LANES), lambda c, b: (b, ucb + c)),
            pl.BlockSpec((L, LANES), lambda c, b: (b, gcb + c)),
            pl.BlockSpec((1, nk1, 2 * P, LANES), lambda c, b: (order, 0, 0, c)),
            _resident((P, mr, R), lambda c, b: (0, 0, 0)),
            _resident((P, R, mr), lambda c, b: (0, 0, 0)),
            _resident((2 * P, 2 * P), lambda c, b: (0, 0)),
            _resident((2 * P, 2 * P), lambda c, b: (0, 0)),
            pl.BlockSpec((3, LANES), lambda c, b: (0, ucb + c if conv_u else c)),
            pl.BlockSpec((1, LANES), lambda c, b: (0, ucb + c if conv_u else c)),
            pl.BlockSpec((3, LANES), lambda c, b: (0, gcb + c)),
            pl.BlockSpec((1, LANES), lambda c, b: (0, gcb + c)),
            pl.BlockSpec((1, LANES), lambda c, b: (0, c)),
        ],
        out_specs=pl.BlockSpec((L, LANES), lambda c, b: (b, c)),
        out_shape=jax.ShapeDtypeStruct((B * L, C), BF16),
        scratch_shapes=[
            pltpu.VMEM((R, P, LANES), F32),
            pltpu.VMEM((mr, P, LANES), F32),
            pltpu.VMEM((R, P, LANES), F32),
        ],
        compiler_params=_cparams("parallel", "arbitrary"),
        name=f"hy_conv{order}",
    )(u_arr, gate_arr, kf, tabs["f1"], tabs["g"], tabs["d_fwd"], tabs["d_inv"],
      conv_w, conv_b, conv_w, conv_b, bias[order].reshape(1, C))


def _attn_kernel(q_ref, k_ref, v_ref, lamv_ref, sg_ref, o_ref, vt_scr, m_scr, acc_scr,
                 *score_bufs, lam_init):
    n_kv, vrows, tk = vt_scr.shape
    dv = v_ref.shape[1]
    d = ATT_HEAD_DIM

    @pl.when(pl.program_id(2) == 0)
    def _():
        for c in range(n_kv):
            vt_scr[c, 0:dv, :] = v_ref[c * tk:(c + 1) * tk, :].astype(F32).T.astype(BF16)
            vt_scr[c, dv:vrows, :] = jnp.ones((vrows - dv, tk), BF16)

    qt = q_ref[...].astype(F32).T
    row = lax.broadcasted_iota(jnp.int32, (2 * d, 1), 0)
    qts = (jnp.where(row < d, qt, 0.0).astype(BF16), jnp.where(row >= d, qt, 0.0).astype(BF16))
    m_scr[...] = jnp.full(m_scr.shape, -jnp.inf, F32)
    acc_scr[...] = jnp.zeros(acc_scr.shape, F32)

    tq = q_ref.shape[0]

    def scores(j, s_ref, cm_ref, q0):
        kc = k_ref[pl.ds(pl.multiple_of(j * tk, tk), tk), :]
        qs = slice(q0, q0 + ATTN_QUERY_TILE)
        for comp in range(2):
            s = jnp.dot(kc, qts[comp][:, qs], preferred_element_type=F32)
            s_ref[comp, :, qs] = s
            cm_ref[comp, :, qs] = jnp.max(s, axis=0, keepdims=True)

    def accumulate(j, s_ref, cm_ref, q0):
        vt = vt_scr[j]
        qs = slice(q0, q0 + ATTN_QUERY_TILE)
        for comp in range(2):
            m_old = m_scr[comp, :, qs]
            m_new = jnp.maximum(m_old, cm_ref[comp, :, qs])
            p = jnp.exp2(s_ref[comp, :, qs] - m_new).astype(BF16)
            acc_scr[comp, :, qs] = (jnp.exp2(m_old - m_new) * acc_scr[comp, :, qs]
                                    + jnp.dot(vt, p, preferred_element_type=F32))
            m_scr[comp, :, qs] = m_new

    def step(score_job, acc_job):
        for q0 in range(0, tq, ATTN_QUERY_TILE):
            if score_job is not None:
                scores(*score_job, q0)
            if acc_job is not None:
                accumulate(*acc_job, q0)

    nbuf = len(score_bufs) // 2
    bufs = [(score_bufs[i], score_bufs[nbuf + i]) for i in range(nbuf)]
    depth = nbuf - 1
    for c in range(min(depth, n_kv)):
        step((c, *bufs[c % nbuf]), None)
    trips = max((n_kv - depth) // nbuf, 0)

    def body(jj, _):
        j = jj * nbuf
        for t in range(nbuf):
            step((j + t + depth, *bufs[(t + depth) % nbuf]), (j + t, *bufs[t]))
        return 0

    if trips > 0:
        lax.fori_loop(0, trips, body, 0, unroll=ATTN_TRIP_UNROLL)
    for c in range(trips * nbuf, n_kv):
        later = (c + depth, *bufs[(c + depth) % nbuf]) if c + depth < n_kv else None
        step(later, (c, *bufs[c % nbuf]))

    lv = lamv_ref[...]
    lam = (jnp.exp(jnp.sum(lv[0:1] * lv[1:2], axis=-1, keepdims=True))
           - jnp.exp(jnp.sum(lv[2:3] * lv[3:4], axis=-1, keepdims=True)) + lam_init)
    a1, a2 = acc_scr[0], acc_scr[1]
    o = a1[0:dv] / a1[dv:dv + 1] - lam * (a2[0:dv] / a2[dv:dv + 1])
    ms = jnp.mean(o * o, axis=0, keepdims=True)
    o = o * lax.rsqrt(ms + EPS) * sg_ref[...] * (1.0 - lam_init)
    o_ref[...] = o.T.astype(o_ref.dtype)


def _diff_attention(proj, q_col0, k_col0, v_col0, lam_vecs, subln_g, lam_init, B, L):
    tq = min(2048, L)
    tk = min(512, L)
    hw = ATT_V_DIM
    vrows = hw + 16
    qcb, kcb, vcb = q_col0 // hw, k_col0 // hw, v_col0 // hw
    nq = L // tq
    kern = functools.partial(_attn_kernel, lam_init=lam_init)
    return pl.pallas_call(
        kern,
        grid=(B, ATT_HEADS, nq),
        in_specs=[
            pl.BlockSpec((tq, hw), lambda b, h, i: (b * nq + i, qcb + h)),
            pl.BlockSpec((L, hw), lambda b, h, i: (b, kcb + h)),
            pl.BlockSpec((L, hw), lambda b, h, i: (b, vcb + h)),
            pl.BlockSpec((4, ATT_HEAD_DIM), lambda b, h, i: (0, 0)),
            pl.BlockSpec((hw, 1), lambda b, h, i: (0, 0)),
        ],
        out_specs=pl.BlockSpec((tq, hw), lambda b, h, i: (b * nq + i, h)),
        out_shape=jax.ShapeDtypeStruct((B * L, ATT_HEADS * hw), BF16),
        scratch_shapes=[
            pltpu.VMEM((L // tk, vrows, tk), BF16),
            pltpu.VMEM((2, 1, tq), F32),
            pltpu.VMEM((2, vrows, tq), F32),
        ] + [pltpu.VMEM((2, tk, tq), F32)] * ATTN_SCORE_BUFS + [pltpu.VMEM((2, 1, tq), F32)] * ATTN_SCORE_BUFS,
        compiler_params=_cparams("parallel", "parallel", "arbitrary"),
        name="diff_attn",
    )(proj, proj, proj, lam_vecs, subln_g.reshape(hw, 1))


def _merge_kernel(yh_ref, ya_ref, gh_ref, ga_ref, wph_ref, wpa_ref, wo_ref, x_ref, mod_ref, o_ref):
    mh = jnp.dot(yh_ref[...], wph_ref[...], preferred_element_type=F32)
    ma = jnp.dot(ya_ref[...], wpa_ref[...], preferred_element_type=F32)
    merged = (gh_ref[...].astype(F32) * mh + ga_ref[...].astype(F32) * ma).astype(BF16)
    r = jnp.dot(merged, wo_ref[...], preferred_element_type=F32)
    o_ref[...] = x_ref[...] + mod_ref[0, 2:3, :] * r


def _merge_out(y_hy, y_att, proj, gate_col0, wph, wpa, wo, x2, mod, seq_len):
    T, D = x2.shape
    tm = 512
    bps = seq_len // tm
    gcb = gate_col0 // D
    cw = y_hy.shape[1]
    return pl.pallas_call(
        _merge_kernel,
        grid=(T // tm,),
        in_specs=[
            pl.BlockSpec((tm, cw), lambda i: (i, 0)),
            pl.BlockSpec((tm, cw), lambda i: (i, 0)),
            pl.BlockSpec((tm, D), lambda i: (i, gcb)),
            pl.BlockSpec((tm, D), lambda i: (i, gcb + 1)),
            _resident((cw, D), lambda i: (0, 0)),
            _resident((cw, D), lambda i: (0, 0)),
            _resident((D, D), lambda i: (0, 0)),
            pl.BlockSpec((tm, D), lambda i: (i, 0)),
            pl.BlockSpec((1, N_MOD, D), lambda i: (i // bps, 0, 0)),
        ],
        out_specs=pl.BlockSpec((tm, D), lambda i: (i, 0)),
        out_shape=jax.ShapeDtypeStruct((T, D), F32),
        compiler_params=_cparams("parallel"),
        name="merge_out",
    )(y_hy, y_att, proj, proj, wph, wpa, wo, x2, mod)


def _ffn_kernel(x_ref, mod_ref, g_ref, wg_ref, wu_ref, wd_ref, o_ref, h_scr, acc_scr):
    j = pl.program_id(1)
    tm = h_scr.shape[0]
    th = wd_ref.shape[0]

    def swiglu_down(h):
        down = None
        for n0 in range(0, th, FFN_HIDDEN_TILE):
            ns = slice(n0, n0 + FFN_HIDDEN_TILE)
            a = jnp.dot(h, wg_ref[:, ns], preferred_element_type=F32)
            u = jnp.dot(h, wu_ref[:, ns], preferred_element_type=F32)
            f = (a * jax.nn.sigmoid(a) * u).astype(BF16)
            part = jnp.dot(f, wd_ref[ns, :], preferred_element_type=F32)
            down = part if down is None else down + part
        return down

    @pl.when(j == 0)
    def _():
        for r in range(tm // FFN_ROW_CHUNK):
            rows = pl.ds(r * FFN_ROW_CHUNK, FFN_ROW_CHUNK)
            h = _modulated_norm(x_ref[rows, :], g_ref[...], mod_ref[0, 3:4, :], mod_ref[0, 4:5, :]).astype(BF16)
            h_scr[rows, :] = h
            acc_scr[rows, :] = swiglu_down(h)

    @pl.when(j > 0)
    def _():
        acc_scr[...] += swiglu_down(h_scr[...])

    @pl.when(j == pl.num_programs(1) - 1)
    def _():
        o_ref[...] = x_ref[...] + mod_ref[0, 5:6, :] * acc_scr[...]


def _ffn(x2, mod, norm_g, wg, wu, wd, seq_len):
    T, D = x2.shape
    H = wg.shape[1]
    tm, th = 512, 512
    bps = seq_len // tm
    return pl.pallas_call(
        _ffn_kernel,
        grid=(T // tm, H // th),
        in_specs=[
            pl.BlockSpec((tm, D), lambda i, j: (i, 0)),
            pl.BlockSpec((1, N_MOD, D), lambda i, j: (i // bps, 0, 0)),
            pl.BlockSpec((1, D), lambda i, j: (0, 0)),
            pl.BlockSpec((D, th), lambda i, j: (0, j)),
            pl.BlockSpec((D, th), lambda i, j: (0, j)),
            pl.BlockSpec((th, D), lambda i, j: (j, 0)),
        ],
        out_specs=pl.BlockSpec((tm, D), lambda i, j: (i, 0)),
        out_shape=jax.ShapeDtypeStruct((T, D), F32),
        scratch_shapes=[pltpu.VMEM((tm, D), BF16), pltpu.VMEM((tm, D), F32)],
        compiler_params=_cparams("parallel", "arbitrary"),
        name="ffn",
    )(x2, mod, norm_g.reshape(1, D), wg, wu, wd)


def kernel(x, c, positions, w_ada, b_ada, norm1_g, w_in, hy_conv_w, hy_conv_b, hy_filt_w1, hy_filt_b1, hy_filt_w2, hy_filt_b2, hy_filt_w3, hy_filt_b3, hy_filt_freq, hy_filt_w_out, hy_bias, q_norm_g, k_norm_g, lam_q1, lam_k1, lam_q2, lam_k2, subln_g, w_proj_hy, w_proj_att, w_out, norm2_g, w_gate, w_up, w_down):
    B, L, D = x.shape
    T = B * L
    depth = w_ada.shape[0]
    s_hy = (HY_ORDER + 1) * HY_WIDTH
    qk_w = ATT_HEADS * 2 * ATT_HEAD_DIM
    q0, k0 = s_hy, s_hy + qk_w
    v0 = k0 + qk_w
    g0 = v0 + ATT_HEADS * ATT_V_DIM
    tabs = _dft_tables(L)
    pos_col = positions.astype(F32).reshape(T, 1)
    x2 = x.reshape(T, D)
    for l in range(depth):
        lam_init = 0.8 - 0.6 * math.exp(-0.3 * l)
        mod = _ada_mod(c, w_ada[l], b_ada[l])
        proj = _in_proj(x2, mod, norm1_g[l], w_in[l].astype(BF16), pos_col,
                        q_norm_g[l], k_norm_g[l], L)

        filt = _hyena_filters(L, hy_filt_w1[l], hy_filt_b1[l], hy_filt_w2[l], hy_filt_b2[l],
                              hy_filt_w3[l], hy_filt_b3[l], hy_filt_freq[l], hy_filt_w_out[l])
        kf = _filter_spectra(filt, tabs)
        cw, cb = hy_conv_w[l], hy_conv_b[l].reshape(1, s_hy)
        z = _hyena_conv(proj, 0, proj, HY_WIDTH, kf, 0, cw, cb, hy_bias[l], tabs, B, L, True)
        y_hy = _hyena_conv(z, 0, proj, 2 * HY_WIDTH, kf, 1, cw, cb, hy_bias[l], tabs, B, L, False)

        lam_vecs = jnp.stack([lam_q1[l], lam_k1[l], lam_q2[l], lam_k2[l]]).astype(F32)
        y_att = _diff_attention(proj, q0, k0, v0, lam_vecs, subln_g[l], lam_init, B, L)

        x2 = _merge_out(y_hy, y_att, proj, g0, w_proj_hy[l].astype(BF16), w_proj_att[l].astype(BF16),
                        w_out[l].astype(BF16), x2, mod, L)
        x2 = _ffn(x2, mod, norm2_g[l], w_gate[l].astype(BF16), w_up[l].astype(BF16),
                  w_down[l].astype(BF16), L)
    return x2.reshape(B, L, D)
```
